```python
import math
import jax, jax.numpy as jnp
from jax import lax
import numpy as np


D_MODEL = 1024
BATCH = 32
SEQ = 2048
DEPTH = 2

N_MIXERS = 2
N_HEADS = 8
HEAD_DIM = 64
V_DIM = 2 * HEAD_DIM
Q_BLOCK = 128
ROPE_THETA = 10000.0
CONV_WIDTH = 3
D_FF = 2816
RMS_EPS = 1e-5
N_ATTN_LAYERS = (DEPTH + N_MIXERS - 1) // N_MIXERS
N_CONV_LAYERS = (DEPTH + N_MIXERS - 2) // N_MIXERS

kernel_name = 'hybrid_diffattn_shortconv_encoder'


def rms_norm(x, g):
    xf = x.astype(jnp.float32)
    y = xf * lax.rsqrt(jnp.mean(xf * xf, axis=-1, keepdims=True) + RMS_EPS)
    return (y * g.astype(jnp.float32)).astype(x.dtype)


def dwconv3(x, w):
    xp = jnp.pad(x, ((0, 0), (1, 1), (0, 0)))
    return xp[:, :-2] * w[0] + xp[:, 1:-1] * w[1] + xp[:, 2:] * w[2]


def rotary_cos_sin(positions, dim):
    inv_freq = ROPE_THETA ** (-jnp.arange(0, dim, 2, dtype=jnp.float32) / dim)
    ang = positions.astype(jnp.float32)[..., None] * inv_freq
    return jnp.cos(ang), jnp.sin(ang)


def apply_rotary(x, cos, sin):
    x1, x2 = jnp.split(x.astype(jnp.float32), 2, axis=-1)
    c = cos[:, :, None, :]
    s = sin[:, :, None, :]
    return jnp.concatenate([x1 * c - x2 * s, x2 * c + x1 * s], axis=-1).astype(x.dtype)


def diff_attention(h, positions, w_qkv, lq1, lk1, lq2, lk2, subln_g, w_o, lambda_init):
    B, S, _ = h.shape
    q, k, v = jnp.split(h @ w_qkv, 3, axis=-1)
    q = q.reshape(B, S, 2 * N_HEADS, HEAD_DIM)
    k = k.reshape(B, S, 2 * N_HEADS, HEAD_DIM)
    v = v.reshape(B, S, N_HEADS, V_DIM)
    cos, sin = rotary_cos_sin(positions, HEAD_DIM)
    q = apply_rotary(q, cos, sin) * (HEAD_DIM ** -0.5)
    k = apply_rotary(k, cos, sin)
    q = q.reshape(B, S, N_HEADS, 2, HEAD_DIM)
    k = k.reshape(B, S, N_HEADS, 2, HEAD_DIM)
    lq1f, lk1f = lq1.astype(jnp.float32), lk1.astype(jnp.float32)
    lq2f, lk2f = lq2.astype(jnp.float32), lk2.astype(jnp.float32)
    lam = jnp.exp(jnp.sum(lq1f * lk1f)) - jnp.exp(jnp.sum(lq2f * lk2f)) + lambda_init
    nb = S // Q_BLOCK
    qb = q.reshape(B, nb, Q_BLOCK, N_HEADS, 2, HEAD_DIM).transpose(1, 0, 2, 3, 4, 5)

    def attend(q_blk):
        s = jnp.einsum('bqhcd,bkhcd->bhcqk', q_blk, k, preferred_element_type=jnp.float32)
        p = jax.nn.softmax(s, axis=-1)
        a = p[:, :, 0] - lam * p[:, :, 1]
        return jnp.einsum('bhqk,bkhe->bqhe', a.astype(v.dtype), v)

    o = lax.map(attend, qb)
    o = o.transpose(1, 0, 2, 3, 4).reshape(B, S, N_HEADS, V_DIM)
    o = rms_norm(o, subln_g) * (1.0 - lambda_init)
    return o.reshape(B, S, N_HEADS * V_DIM) @ w_o


def short_conv_mixer(h, w_in, conv_w, w_out):
    b, c, u = jnp.split(h @ w_in, 3, axis=-1)
    return (b * dwconv3(c * u, conv_w)) @ w_out


def conv_ffn(h, w_up, conv_w, conv_b, w_down):
    u = dwconv3(h @ w_up, conv_w) + conv_b
    gate, val = jnp.split(u, 2, axis=-1)
    return (jax.nn.silu(gate) * val) @ w_down


def setup_inputs(seed: int = 0) -> dict:
    key = jax.random.key(seed)
    ks = jax.random.split(key, 24)

    def nrm(k, shape, scale):
        return jax.random.normal(k, shape, dtype=jnp.float32) * scale

    NA, NC, L = N_ATTN_LAYERS, N_CONV_LAYERS, DEPTH
    x = nrm(ks[0], (BATCH, SEQ, D_MODEL), 1.0)
    positions = jnp.broadcast_to(jnp.arange(SEQ, dtype=jnp.int32), (BATCH, SEQ))
    return {
        'x': x,
        'positions': positions,
        'attn_norm_g': 1.0 + nrm(ks[1], (NA, D_MODEL), 0.02),
        'attn_w_qkv': nrm(ks[2], (NA, D_MODEL, 3 * D_MODEL), D_MODEL ** -0.5),
        'attn_lambda_q1': nrm(ks[3], (NA, HEAD_DIM), 0.1),
        'attn_lambda_k1': nrm(ks[4], (NA, HEAD_DIM), 0.1),
        'attn_lambda_q2': nrm(ks[5], (NA, HEAD_DIM), 0.1),
        'attn_lambda_k2': nrm(ks[6], (NA, HEAD_DIM), 0.1),
        'attn_subln_g': 1.0 + nrm(ks[7], (NA, V_DIM), 0.02),
        'attn_w_o': nrm(ks[8], (NA, D_MODEL, D_MODEL), D_MODEL ** -0.5),
        'conv_norm_g': 1.0 + nrm(ks[9], (NC, D_MODEL), 0.02),
        'conv_w_in': nrm(ks[10], (NC, D_MODEL, 3 * D_MODEL), D_MODEL ** -0.5),
        'conv_w': nrm(ks[11], (NC, CONV_WIDTH, D_MODEL), CONV_WIDTH ** -0.5),
        'conv_w_out': nrm(ks[12], (NC, D_MODEL, D_MODEL), D_MODEL ** -0.5),
        'ffn_norm_g': 1.0 + nrm(ks[13], (L, D_MODEL), 0.02),
        'ffn_w_up': nrm(ks[14], (L, D_MODEL, 2 * D_FF), D_MODEL ** -0.5),
        'ffn_conv_w': nrm(ks[15], (L, CONV_WIDTH, 2 * D_FF), CONV_WIDTH ** -0.5),
        'ffn_conv_b': nrm(ks[16], (L, 2 * D_FF), 0.02),
        'ffn_w_down': nrm(ks[17], (L, D_FF, D_MODEL), D_FF ** -0.5),
        'final_norm_g': 1.0 + nrm(ks[18], (D_MODEL,), 0.02),
    }


def reference(x, positions, attn_norm_g, attn_w_qkv, attn_lambda_q1, attn_lambda_k1, attn_lambda_q2, attn_lambda_k2, attn_subln_g, attn_w_o, conv_norm_g, conv_w_in, conv_w, conv_w_out, ffn_norm_g, ffn_w_up, ffn_conv_w, ffn_conv_b, ffn_w_down, final_norm_g):
    for i in range(DEPTH):
        if i % N_MIXERS == 0:
            a = i // N_MIXERS
            lambda_init = 0.8 - 0.6 * math.exp(-0.3 * i)
            h = rms_norm(x, attn_norm_g[a])
            x = x + diff_attention(h, positions, attn_w_qkv[a], attn_lambda_q1[a], attn_lambda_k1[a],
                                   attn_lambda_q2[a], attn_lambda_k2[a], attn_subln_g[a], attn_w_o[a],
                                   lambda_init)
        else:
            c = i // N_MIXERS
            h = rms_norm(x, conv_norm_g[c])
            x = x + short_conv_mixer(h, conv_w_in[c], conv_w[c], conv_w_out[c])
        h = rms_norm(x, ffn_norm_g[i])
        x = x + conv_ffn(h, ffn_w_up[i], ffn_conv_w[i], ffn_conv_b[i], ffn_w_down[i])
    return rms_norm(x, final_norm_g)
```

```python
import functools
import math

import jax
import jax.numpy as jnp
from jax import lax
from jax.experimental import pallas as pl
from jax.experimental.pallas import tpu as pltpu

N_HEADS = 8
HEAD_DIM = 64
V_DIM = 2 * HEAD_DIM
ROPE_THETA = 10000.0
RMS_EPS = 1e-5
N_MIXERS = 2

F32 = jnp.float32
BF16 = jnp.bfloat16

LANES = 128
BF16_SUBLANES = 16
VMEM_LIMIT_BYTES = 56 * 1024 * 1024

QKV_ROWS = 512
ATTN_Q_ROWS = 256
BLOCK_ROWS = 1024
BLOCK_COLS = 256
CONV_CHUNK = 256
HALO = BF16_SUBLANES


def _rms(x, g):
    return x * lax.rsqrt(jnp.mean(x * x, axis=-1, keepdims=True) + RMS_EPS) * g


def _qkv_kernel(x_ref, pos_ref, g_ref, invf_ref, w_ref, q_ref, k_ref, v_ref):
    d = x_ref.shape[1]
    h = _rms(x_ref[...], g_ref[...]).astype(BF16)
    acc = jnp.dot(h, w_ref[...], preferred_element_type=F32)

    ang = pos_ref[...].astype(F32) * invf_ref[...]
    cos = jnp.cos(ang)
    sin = jnp.sin(ang)
    lane = lax.broadcasted_iota(jnp.int32, (1, LANES), 1)
    first_half = (lane & (HEAD_DIM // 2)) == 0
    sin_signed = jnp.where(first_half, -sin, sin)
    q_scale = HEAD_DIM ** -0.5
    cos_q, sin_q = cos * q_scale, sin_signed * q_scale

    n_groups = d // LANES
    for g in range(2 * n_groups):
        xg = acc[:, g * LANES:(g + 1) * LANES]
        swapped = jnp.where(first_half,
                            pltpu.roll(xg, LANES - HEAD_DIM // 2, 1),
                            pltpu.roll(xg, HEAD_DIM // 2, 1))
        if g < n_groups:
            q_ref[:, g * LANES:(g + 1) * LANES] = (xg * cos_q + swapped * sin_q).astype(BF16)
        else:
            c0 = (g - n_groups) * LANES
            k_ref[:, c0:c0 + LANES] = (xg * cos + swapped * sin_signed).astype(BF16)
    v_ref[...] = acc[:, 2 * d:].astype(BF16)


def _qkv_proj(x2, pos2, g, w_qkv):
    t, d = x2.shape
    rows = QKV_ROWS
    inv_freq = ROPE_THETA ** (-jnp.arange(0, HEAD_DIM, 2, dtype=F32) / HEAD_DIM)
    invf = jnp.tile(inv_freq, LANES // (HEAD_DIM // 2)).reshape(1, LANES)
    out = jax.ShapeDtypeStruct((t, d), BF16)
    row_spec = pl.BlockSpec((rows, d), lambda i: (i, 0))
    return pl.pallas_call(
        _qkv_kernel,
        grid=(t // rows,),
        in_specs=[
            row_spec,
            pl.BlockSpec((rows, 1), lambda i: (i, 0)),
            pl.BlockSpec((1, d), lambda i: (0, 0)),
            pl.BlockSpec((1, LANES), lambda i: (0, 0)),
            pl.BlockSpec((d, 3 * d), lambda i: (0, 0)),
        ],
        out_specs=[row_spec, row_spec, row_spec],
        out_shape=[out, out, out],
        compiler_params=pltpu.CompilerParams(
            dimension_semantics=("arbitrary",), vmem_limit_bytes=VMEM_LIMIT_BYTES),
        name="qkv_rope",
    )(x2, pos2, g.reshape(1, d), invf, w_qkv.astype(BF16))


def _attn_kernel(q_ref, k_ref, v_ref, x_ref, wo_ref, sg_ref, lq1_ref, lk1_ref,
                 lq2_ref, lk2_ref, o_ref, heads_ref, *, lambda_init):
    tq = q_ref.shape[0]
    lam = (jnp.exp(jnp.sum(lq1_ref[...] * lk1_ref[...], axis=1, keepdims=True))
           - jnp.exp(jnp.sum(lq2_ref[...] * lk2_ref[...], axis=1, keepdims=True))
           + lambda_init)
    lane = lax.broadcasted_iota(jnp.int32, (1, LANES), 1)
    map1 = lane < HEAD_DIM
    sub_g = sg_ref[...] * (1.0 - lambda_init)

    for h in range(N_HEADS):
        cols = slice(h * V_DIM, (h + 1) * V_DIM)
        qh = q_ref[:, cols]
        zero = jnp.zeros_like(qh)
        q_both = jnp.concatenate([jnp.where(map1, qh, zero), jnp.where(map1, zero, qh)], axis=0)
        s = lax.dot_general(q_both, k_ref[:, cols], (((1,), (1,)), ((), ())),
                            preferred_element_type=F32)
        e = jnp.exp(s - jnp.max(s, axis=1, keepdims=True))
        denom = jnp.sum(e, axis=1, keepdims=True)
        pv = jnp.dot(e.astype(BF16), v_ref[:, cols], preferred_element_type=F32)
        pv = pv / denom
        o = pv[:tq] - lam * pv[tq:]
        heads_ref[:, cols] = _rms(o, sub_g).astype(BF16)

    o_ref[...] = x_ref[...] + jnp.dot(heads_ref[...], wo_ref[...], preferred_element_type=F32)


def _attention(x2, q, k, v, w_o, sub_g, lq1, lk1, lq2, lk2, *, seq, lambda_init):
    t, d = x2.shape
    tq = ATTN_Q_ROWS
    nq = seq // tq
    q_spec = pl.BlockSpec((tq, d), lambda b, i: (b * nq + i, 0))
    kv_spec = pl.BlockSpec((seq, d), lambda b, i: (b, 0))
    vec = lambda n: pl.BlockSpec((1, n), lambda b, i: (0, 0))
    return pl.pallas_call(
        functools.partial(_attn_kernel, lambda_init=lambda_init),
        grid=(t // seq, nq),
        in_specs=[q_spec, kv_spec, kv_spec, q_spec,
                  pl.BlockSpec((d, d), lambda b, i: (0, 0)),
                  vec(V_DIM), vec(HEAD_DIM), vec(HEAD_DIM), vec(HEAD_DIM), vec(HEAD_DIM)],
        out_specs=q_spec,
        out_shape=jax.ShapeDtypeStruct((t, d), F32),
        scratch_shapes=[pltpu.VMEM((tq, d), BF16)],
        compiler_params=pltpu.CompilerParams(
            dimension_semantics=("arbitrary", "arbitrary"), vmem_limit_bytes=VMEM_LIMIT_BYTES),
        name="diff_attn",
    )(q, k, v, x2, w_o.astype(BF16), sub_g.reshape(1, V_DIM),
      lq1.reshape(1, HEAD_DIM), lk1.reshape(1, HEAD_DIM),
      lq2.reshape(1, HEAD_DIM), lk2.reshape(1, HEAD_DIM))


def _gated_block_kernel(*refs, n_up, mixer, final_norm, blocks_per_seq):
    x_ref, xprev_ref, xnext_ref, g_ref = refs[:4]
    up_refs = refs[4:4 + n_up]
    pos = 4 + n_up
    if mixer:
        cw_refs, cb_refs = (refs[pos],), ()
        pos += 1
    else:
        cw_refs, cb_refs = refs[pos:pos + 2], refs[pos + 2:pos + 4]
        pos += 4
    wd_ref = refs[pos]
    pos += 1
    if final_norm:
        fg_ref = refs[pos]
        pos += 1
    o_ref, h_ref, u_ref, z_ref = refs[pos:pos + 4]

    rows = x_ref.shape[0]
    cols = wd_ref.shape[0]
    i = pl.program_id(0)
    j = pl.program_id(1)

    @pl.when(j == 0)
    def _():
        x = x_ref[...]
        g = g_ref[...]
        r = i % blocks_per_seq
        h_prev = jnp.where(r == 0, 0.0, _rms(xprev_ref[...], g))
        h_next = jnp.where(r == blocks_per_seq - 1, 0.0, _rms(xnext_ref[...], g))
        h_ref[0:HALO, :] = h_prev.astype(BF16)
        h_ref[HALO:HALO + rows, :] = _rms(x, g).astype(BF16)
        h_ref[HALO + rows:, :] = h_next.astype(BF16)
        o_ref[...] = x

    h = h_ref[...]
    for n, w_ref in enumerate(up_refs):
        u_ref[:, n * cols:(n + 1) * cols] = jnp.dot(h, w_ref[...], preferred_element_type=F32)

    def taps(r0, n):
        c0 = n * cols
        return [u_ref[r0 - 1 + t:r0 - 1 + t + CONV_CHUNK, c0:c0 + cols] for t in range(3)]

    def conv(vals, cw_ref, cb_ref=None):
        y = cw_ref[0:1, :] * vals[0] + cw_ref[1:2, :] * vals[1] + cw_ref[2:3, :] * vals[2]
        return y if cb_ref is None else y + cb_ref[...]

    for c in range(rows // CONV_CHUNK):
        r0 = HALO + c * CONV_CHUNK
        if mixer:
            gate = u_ref[r0:r0 + CONV_CHUNK, 0:cols]
            cu = [a * b for a, b in zip(taps(r0, 1), taps(r0, 2))]
            z = gate * conv(cu, cw_refs[0])
        else:
            yg = conv(taps(r0, 0), cw_refs[0], cb_refs[0])
            yv = conv(taps(r0, 1), cw_refs[1], cb_refs[1])
            z = yg * jax.nn.sigmoid(yg) * yv
        z_ref[c * CONV_CHUNK:(c + 1) * CONV_CHUNK, :] = z.astype(BF16)

    o_ref[...] += jnp.dot(z_ref[...], wd_ref[...], preferred_element_type=F32)

    if final_norm:
        @pl.when(j == pl.num_programs(1) - 1)
        def _():
            o_ref[...] = _rms(o_ref[...], fg_ref[...])


def _gated_block(x2, norm_g, w_up, conv_w, conv_b, w_down, final_g, *, seq, mixer):
    t, d = x2.shape
    hidden = w_down.shape[0]
    n_up = w_up.shape[1] // hidden
    rows, cols = BLOCK_ROWS, BLOCK_COLS
    n_col = hidden // cols
    blocks_per_seq = seq // rows
    halo_per_block = rows // HALO
    last_halo = t // HALO - 1
    final_norm = final_g is not None

    row_spec = pl.BlockSpec((rows, d), lambda i, j: (i, 0))
    prev_spec = pl.BlockSpec((HALO, d), lambda i, j: (jnp.maximum(i * halo_per_block - 1, 0), 0))
    next_spec = pl.BlockSpec((HALO, d), lambda i, j: (jnp.minimum((i + 1) * halo_per_block, last_halo), 0))

    def col_tile(n_rows, part):
        return pl.BlockSpec((n_rows, cols), lambda i, j: (0, part * n_col + j))

    w_up_b = w_up.astype(BF16)
    operands = [x2, x2, x2, norm_g.reshape(1, d)]
    in_specs = [row_spec, prev_spec, next_spec, pl.BlockSpec((1, d), lambda i, j: (0, 0))]
    for part in range(n_up):
        operands.append(w_up_b)
        in_specs.append(col_tile(d, part))
    if mixer:
        operands.append(conv_w)
        in_specs.append(col_tile(3, 0))
    else:
        conv_b2 = conv_b.reshape(1, -1)
        operands += [conv_w, conv_w, conv_b2, conv_b2]
        in_specs += [col_tile(3, 0), col_tile(3, 1), col_tile(1, 0), col_tile(1, 1)]
    operands.append(w_down.astype(BF16))
    in_specs.append(pl.BlockSpec((cols, d), lambda i, j: (j, 0)))
    if final_norm:
        operands.append(final_g.reshape(1, d))
        in_specs.append(pl.BlockSpec((1, d), lambda i, j: (0, 0)))

    return pl.pallas_call(
        functools.partial(_gated_block_kernel, n_up=n_up, mixer=mixer,
                          final_norm=final_norm, blocks_per_seq=blocks_per_seq),
        grid=(t // rows, n_col),
        in_specs=in_specs,
        out_specs=row_spec,
        out_shape=jax.ShapeDtypeStruct((t, d), F32),
        scratch_shapes=[
            pltpu.VMEM((rows + 2 * HALO, d), BF16),
            pltpu.VMEM((rows + 2 * HALO, n_up * cols), F32),
            pltpu.VMEM((rows, cols), BF16),
        ],
        compiler_params=pltpu.CompilerParams(
            dimension_semantics=("arbitrary", "arbitrary"), vmem_limit_bytes=VMEM_LIMIT_BYTES),
        name="mixer_block" if mixer else "conv_ffn",
    )(*operands)


def kernel(x, positions, attn_norm_g, attn_w_qkv, attn_lambda_q1, attn_lambda_k1, attn_lambda_q2, attn_lambda_k2, attn_subln_g, attn_w_o, conv_norm_g, conv_w_in, conv_w, conv_w_out, ffn_norm_g, ffn_w_up, ffn_conv_w, ffn_conv_b, ffn_w_down, final_norm_g):
    batch, seq, d = x.shape
    depth = ffn_norm_g.shape[0]
    assert d == N_HEADS * V_DIM
    x2 = x.reshape(batch * seq, d)
    pos2 = positions.reshape(batch * seq, 1)
    for i in range(depth):
        if i % N_MIXERS == 0:
            a = i // N_MIXERS
            lambda_init = 0.8 - 0.6 * math.exp(-0.3 * i)
            q, k, v = _qkv_proj(x2, pos2, attn_norm_g[a], attn_w_qkv[a])
            x2 = _attention(x2, q, k, v, attn_w_o[a], attn_subln_g[a],
                            attn_lambda_q1[a], attn_lambda_k1[a],
                            attn_lambda_q2[a], attn_lambda_k2[a],
                            seq=seq, lambda_init=lambda_init)
        else:
            c = i // N_MIXERS
            x2 = _gated_block(x2, conv_norm_g[c], conv_w_in[c], conv_w[c], None,
                              conv_w_out[c], None, seq=seq, mixer=True)
        x2 = _gated_block(x2, ffn_norm_g[i], ffn_w_up[i], ffn_conv_w[i], ffn_conv_b[i],
                          ffn_w_down[i], final_norm_g if i == depth - 1 else None,
                          seq=seq, mixer=False)
    return x2.reshape(batch, seq, d)
```

```python
import functools
import math

import jax
import jax.numpy as jnp
from jax import lax
from jax.experimental import pallas as pl
from jax.experimental.pallas import tpu as pltpu

N_HEADS = 8
HEAD_DIM = 64
V_DIM = 2 * HEAD_DIM
ROPE_THETA = 10000.0
RMS_EPS = 1e-5
N_MIXERS = 2

F32 = jnp.float32
BF16 = jnp.bfloat16

LANES = 128
SUBLANES = 8
BF16_SUBLANES = 16
VMEM_LIMIT_BYTES = 56 * 1024 * 1024

QKV_ROWS = 512
ATTN_Q_ROWS = 256
BLOCK_ROWS = 1024
BLOCK_COLS = 256
CONV_CHUNK = 256
ACT_ROWS = 256
NORM_ROWS = 32
HALO = BF16_SUBLANES


def _rms(x, g):
    return x * lax.rsqrt(jnp.mean(x * x, axis=-1, keepdims=True) + RMS_EPS) * g


def _qkv_kernel(x_ref, pos_ref, g_ref, invf_ref, w_ref, q_ref, k_ref, v_ref):
    d = x_ref.shape[1]
    h = _rms(x_ref[...], g_ref[...]).astype(BF16)
    acc = jnp.dot(h, w_ref[...], preferred_element_type=F32)

    ang = pos_ref[...].astype(F32) * invf_ref[...]
    cos = jnp.cos(ang)
    sin = jnp.sin(ang)
    lane = lax.broadcasted_iota(jnp.int32, (1, LANES), 1)
    first_half = (lane & (HEAD_DIM // 2)) == 0
    sin_signed = jnp.where(first_half, -sin, sin)
    q_scale = HEAD_DIM ** -0.5
    cos_q, sin_q = cos * q_scale, sin_signed * q_scale

    n_groups = d // LANES
    for g in range(2 * n_groups):
        xg = acc[:, g * LANES:(g + 1) * LANES]
        swapped = jnp.where(first_half,
                            pltpu.roll(xg, LANES - HEAD_DIM // 2, 1),
                            pltpu.roll(xg, HEAD_DIM // 2, 1))
        if g < n_groups:
            q_ref[:, g * LANES:(g + 1) * LANES] = (xg * cos_q + swapped * sin_q).astype(BF16)
        else:
            c0 = (g - n_groups) * LANES
            k_ref[:, c0:c0 + LANES] = (xg * cos + swapped * sin_signed).astype(BF16)
    v_ref[...] = acc[:, 2 * d:].astype(BF16)


def _qkv_proj(x2, pos2, g, w_qkv):
    t, d = x2.shape
    rows = QKV_ROWS
    inv_freq = ROPE_THETA ** (-jnp.arange(0, HEAD_DIM, 2, dtype=F32) / HEAD_DIM)
    invf = jnp.tile(inv_freq, LANES // (HEAD_DIM // 2)).reshape(1, LANES)
    out = jax.ShapeDtypeStruct((t, d), BF16)
    row_spec = pl.BlockSpec((rows, d), lambda i: (i, 0))
    return pl.pallas_call(
        _qkv_kernel,
        grid=(t // rows,),
        in_specs=[
            row_spec,
            pl.BlockSpec((rows, 1), lambda i: (i, 0)),
            pl.BlockSpec((1, d), lambda i: (0, 0)),
            pl.BlockSpec((1, LANES), lambda i: (0, 0)),
            pl.BlockSpec((d, 3 * d), lambda i: (0, 0)),
        ],
        out_specs=[row_spec, row_spec, row_spec],
        out_shape=[out, out, out],
        compiler_params=pltpu.CompilerParams(
            dimension_semantics=("arbitrary",), vmem_limit_bytes=VMEM_LIMIT_BYTES),
        name="qkv_rope",
    )(x2, pos2, g.reshape(1, d), invf, w_qkv.astype(BF16))


def _attn_kernel(q_ref, k_ref, v_ref, x_ref, wo_ref, sg_ref, lq1_ref, lk1_ref,
                 lq2_ref, lk2_ref, o_ref, heads_ref, *, lambda_init):
    tq = q_ref.shape[0]
    lam = (jnp.exp(jnp.sum(lq1_ref[...] * lk1_ref[...], axis=1, keepdims=True))
           - jnp.exp(jnp.sum(lq2_ref[...] * lk2_ref[...], axis=1, keepdims=True))
           + lambda_init)
    lane = lax.broadcasted_iota(jnp.int32, (1, LANES), 1)
    map1 = lane < HEAD_DIM
    sub_g = sg_ref[...] * (1.0 - lambda_init)

    for h in range(N_HEADS):
        cols = slice(h * V_DIM, (h + 1) * V_DIM)
        qh = q_ref[:, cols]
        zero = jnp.zeros_like(qh)
        q_both = jnp.concatenate([jnp.where(map1, qh, zero), jnp.where(map1, zero, qh)], axis=0)
        s = lax.dot_general(q_both, k_ref[:, cols], (((1,), (1,)), ((), ())),
                            preferred_element_type=F32)
        e = jnp.exp(s - jnp.max(s, axis=1, keepdims=True))
        denom = jnp.sum(e, axis=1, keepdims=True)
        pv = jnp.dot(e.astype(BF16), v_ref[:, cols], preferred_element_type=F32)
        pv = pv / denom
        o = pv[:tq] - lam * pv[tq:]
        heads_ref[:, cols] = _rms(o, sub_g).astype(BF16)

    o_ref[...] = x_ref[...] + jnp.dot(heads_ref[...], wo_ref[...], preferred_element_type=F32)


def _attention(x2, q, k, v, w_o, sub_g, lq1, lk1, lq2, lk2, *, seq, lambda_init):
    t, d = x2.shape
    tq = ATTN_Q_ROWS
    nq = seq // tq
    q_spec = pl.BlockSpec((tq, d), lambda b, i: (b * nq + i, 0))
    kv_spec = pl.BlockSpec((seq, d), lambda b, i: (b, 0))
    vec = lambda n: pl.BlockSpec((1, n), lambda b, i: (0, 0))
    return pl.pallas_call(
        functools.partial(_attn_kernel, lambda_init=lambda_init),
        grid=(t // seq, nq),
        in_specs=[q_spec, kv_spec, kv_spec, q_spec,
                  pl.BlockSpec((d, d), lambda b, i: (0, 0)),
                  vec(V_DIM), vec(HEAD_DIM), vec(HEAD_DIM), vec(HEAD_DIM), vec(HEAD_DIM)],
        out_specs=q_spec,
        out_shape=jax.ShapeDtypeStruct((t, d), F32),
        scratch_shapes=[pltpu.VMEM((tq, d), BF16)],
        compiler_params=pltpu.CompilerParams(
            dimension_semantics=("arbitrary", "arbitrary"), vmem_limit_bytes=VMEM_LIMIT_BYTES),
        name="diff_attn",
    )(q, k, v, x2, w_o.astype(BF16), sub_g.reshape(1, V_DIM),
      lq1.reshape(1, HEAD_DIM), lk1.reshape(1, HEAD_DIM),
      lq2.reshape(1, HEAD_DIM), lk2.reshape(1, HEAD_DIM))


def _gated_block_kernel(*refs, n_up, n_col, mixer, final_norm, blocks_per_seq, down_group):
    x_ref, xprev_ref, xnext_ref, g_ref, wup_ref, cw_ref = refs[:6]
    pos = 6
    if not mixer:
        cb_ref = refs[pos]
        pos += 1
    wd_ref = refs[pos]
    pos += 1
    if final_norm:
        fg_ref = refs[pos]
        pos += 1
    o_ref, h_ref = refs[pos:pos + 2]
    u_refs = refs[pos + 2:pos + 4]
    z_refs = refs[pos + 4:pos + 8]

    rows = x_ref.shape[0]
    cols = wd_ref.shape[1]

    n_chunks = rows // CONV_CHUNK
    g = g_ref[...]
    r = pl.program_id(0) % blocks_per_seq
    h_prev = jnp.where(r == 0, 0.0, _rms(xprev_ref[...], g))
    h_next = jnp.where(r == blocks_per_seq - 1, 0.0, _rms(xnext_ref[...], g))
    h_ref[0:HALO, :] = h_prev.astype(BF16)
    h_ref[HALO + rows:, :] = h_next.astype(BF16)
    for c in range(rows // NORM_ROWS):
        x = x_ref[c * NORM_ROWS:(c + 1) * NORM_ROWS, :]
        h_ref[HALO + c * NORM_ROWS:HALO + (c + 1) * NORM_ROWS, :] = _rms(x, g).astype(BF16)
        o_ref[c * NORM_ROWS:(c + 1) * NORM_ROWS, :] = x

    def h_span(c):
        lo = 0 if c == 0 else HALO + c * CONV_CHUNK
        hi = rows + 2 * HALO if c == n_chunks - 1 else HALO + (c + 1) * CONV_CHUNK
        return lo, hi

    def up(j, by_chunk=False):
        spans = [h_span(c) for c in range(n_chunks)] if by_chunk else [(0, rows + 2 * HALO)]
        for lo, hi in spans:
            h = h_ref[lo:hi, :]
            for n in range(n_up):
                u_refs[j % 2][lo:hi, n * cols:(n + 1) * cols] = jnp.dot(
                    h, wup_ref[n * n_col + j], preferred_element_type=F32)

    def act(j):
        u = u_refs[j % 2]

        def window(part, r0):
            return u[r0 - SUBLANES:r0 + ACT_ROWS + SUBLANES, part * cols:(part + 1) * cols]

        def conv(win, part):
            n = win.shape[0]
            cw = cw_ref[part * n_col + j]
            inner = slice(SUBLANES, SUBLANES + ACT_ROWS)
            return (cw[0:1] * pltpu.roll(win, 1, 0)[inner] + cw[1:2] * win[inner]
                    + cw[2:3] * pltpu.roll(win, n - 1, 0)[inner])

        for c in range(rows // ACT_ROWS):
            r0 = HALO + c * ACT_ROWS
            if mixer:
                gate = u[r0:r0 + ACT_ROWS, 0:cols]
                z = gate * conv(window(1, r0) * window(2, r0), 0)
            else:
                yg = conv(window(0, r0), 0) + cb_ref[j]
                yv = conv(window(1, r0), 1) + cb_ref[n_col + j]
                z = yg * jax.nn.sigmoid(yg) * yv
            z_refs[j % len(z_refs)][c * ACT_ROWS:(c + 1) * ACT_ROWS, :] = z.astype(BF16)

    def down(tiles, row_chunk=None, last=False):
        step = rows if row_chunk is None else row_chunk
        for r0 in range(0, rows, step):
            rs = slice(r0, r0 + step)
            acc = o_ref[rs, :]
            for t in tiles:
                acc = acc + jnp.dot(z_refs[t % len(z_refs)][rs, :], wd_ref[t],
                                    preferred_element_type=F32)
            o_ref[rs, :] = acc
            if last and final_norm:
                for p in range(r0, r0 + step, NORM_ROWS):
                    o_ref[p:p + NORM_ROWS, :] = _rms(o_ref[p:p + NORM_ROWS, :], fg_ref[...])

    up(0, by_chunk=True)
    done = 0
    for s in range(n_col):
        if s + 1 < n_col:
            up(s + 1)
        act(s)
        if s >= down_group and s % down_group == 0:
            down(list(range(s - down_group, s)))
            done = s
    down(list(range(done, n_col)), row_chunk=CONV_CHUNK, last=True)


def _gated_block(x2, norm_g, w_up, conv_w, conv_b, w_down, final_g, *, seq, mixer):
    t, d = x2.shape
    hidden = w_down.shape[0]
    n_up = w_up.shape[1] // hidden
    n_conv = conv_w.shape[1] // hidden
    rows, cols = BLOCK_ROWS, BLOCK_COLS
    n_col = hidden // cols
    blocks_per_seq = seq // rows
    halo_per_block = rows // HALO
    last_halo = t // HALO - 1
    final_norm = final_g is not None

    row_spec = pl.BlockSpec((rows, d), lambda i: (i, 0))
    prev_spec = pl.BlockSpec((HALO, d), lambda i: (jnp.maximum(i * halo_per_block - 1, 0), 0))
    next_spec = pl.BlockSpec((HALO, d), lambda i: (jnp.minimum((i + 1) * halo_per_block, last_halo), 0))

    def resident(a):
        zeros = (0,) * a.ndim
        return pl.BlockSpec(a.shape, lambda i: zeros, pipeline_mode=pl.Buffered(1))

    w_up_t = w_up.astype(BF16).reshape(d, n_up * n_col, cols).transpose(1, 0, 2)
    conv_w_t = conv_w.reshape(3, n_conv * n_col, cols).transpose(1, 0, 2)
    w_down_t = w_down.astype(BF16).reshape(n_col, cols, d)
    weights = [norm_g.reshape(1, d), w_up_t, conv_w_t]
    if not mixer:
        weights.append(conv_b.reshape(n_conv * n_col, 1, cols))
    weights.append(w_down_t)
    if final_norm:
        weights.append(final_g.reshape(1, d))

    return pl.pallas_call(
        functools.partial(_gated_block_kernel, n_up=n_up, n_col=n_col, mixer=mixer,
                          final_norm=final_norm, blocks_per_seq=blocks_per_seq,
                          down_group=1 if mixer else 2),
        grid=(t // rows,),
        in_specs=[row_spec, prev_spec, next_spec] + [resident(w) for w in weights],
        out_specs=row_spec,
        out_shape=jax.ShapeDtypeStruct((t, d), F32),
        scratch_shapes=[
            pltpu.VMEM((rows + 2 * HALO, d), BF16),
            pltpu.VMEM((rows + 2 * HALO, n_up * cols), F32),
            pltpu.VMEM((rows + 2 * HALO, n_up * cols), F32),
        ] + [pltpu.VMEM((rows, cols), BF16)] * 4,
        compiler_params=pltpu.CompilerParams(
            dimension_semantics=("arbitrary",), vmem_limit_bytes=VMEM_LIMIT_BYTES),
        name="mixer_block" if mixer else "conv_ffn",
    )(x2, x2, x2, *weights)


def kernel(x, positions, attn_norm_g, attn_w_qkv, attn_lambda_q1, attn_lambda_k1, attn_lambda_q2, attn_lambda_k2, attn_subln_g, attn_w_o, conv_norm_g, conv_w_in, conv_w, conv_w_out, ffn_norm_g, ffn_w_up, ffn_conv_w, ffn_conv_b, ffn_w_down, final_norm_g):
    batch, seq, d = x.shape
    depth = ffn_norm_g.shape[0]
    assert d == N_HEADS * V_DIM
    x2 = x.reshape(batch * seq, d)
    pos2 = positions.reshape(batch * seq, 1)
    for i in range(depth):
        if i % N_MIXERS == 0:
            a = i // N_MIXERS
            lambda_init = 0.8 - 0.6 * math.exp(-0.3 * i)
            q, k, v = _qkv_proj(x2, pos2, attn_norm_g[a], attn_w_qkv[a])
            x2 = _attention(x2, q, k, v, attn_w_o[a], attn_subln_g[a],
                            attn_lambda_q1[a], attn_lambda_k1[a],
                            attn_lambda_q2[a], attn_lambda_k2[a],
                            seq=seq, lambda_init=lambda_init)
        else:
            c = i // N_MIXERS
            x2 = _gated_block(x2, conv_norm_g[c], conv_w_in[c], conv_w[c], None,
                              conv_w_out[c], None, seq=seq, mixer=True)
        x2 = _gated_block(x2, ffn_norm_g[i], ffn_w_up[i], ffn_conv_w[i], ffn_conv_b[i],
                          ffn_w_down[i], final_norm_g if i == depth - 1 else None,
                          seq=seq, mixer=False)
    return x2.reshape(batch, seq, d)
```

```python
import functools
import math

import jax
import jax.numpy as jnp
from jax import lax
from jax.experimental import pallas as pl
from jax.experimental.pallas import tpu as pltpu

N_HEADS = 8
HEAD_DIM = 64
V_DIM = 2 * HEAD_DIM
ROPE_THETA = 10000.0
RMS_EPS = 1e-5
N_MIXERS = 2

F32 = jnp.float32
BF16 = jnp.bfloat16

LANES = 128
SUBLANES = 8
BF16_SUBLANES = 16
VMEM_LIMIT_BYTES = 56 * 1024 * 1024

QKV_ROWS = 512
ATTN_Q_ROWS = 256
BLOCK_ROWS = 1024
BLOCK_COLS = 256
CONV_CHUNK = 256
ACT_ROWS = 256
NORM_ROWS = 32
HALO = BF16_SUBLANES


def _rms(x, g):
    return x * lax.rsqrt(jnp.mean(x * x, axis=-1, keepdims=True) + RMS_EPS) * g


def _qkv_kernel(x_ref, pos_ref, g_ref, invf_ref, wqk_ref, wvt_ref, q_ref, k_ref, vt_ref):
    d = x_ref.shape[1]
    h = _rms(x_ref[...], g_ref[...]).astype(BF16)
    acc = jnp.dot(h, wqk_ref[...], preferred_element_type=F32)
    vt = lax.dot_general(wvt_ref[...], h, (((1,), (1,)), ((), ())), preferred_element_type=F32)
    vt_ref[...] = vt.astype(BF16)

    ang = pos_ref[...].astype(F32) * invf_ref[...]
    cos = jnp.cos(ang)
    sin = jnp.sin(ang)
    lane = lax.broadcasted_iota(jnp.int32, (1, LANES), 1)
    first_half = (lane & (HEAD_DIM // 2)) == 0
    sin_signed = jnp.where(first_half, -sin, sin)
    q_scale = HEAD_DIM ** -0.5 * math.log2(math.e)
    cos_q, sin_q = cos * q_scale, sin_signed * q_scale

    n_groups = d // LANES
    for g in range(2 * n_groups):
        xg = acc[:, g * LANES:(g + 1) * LANES]
        swapped = jnp.where(first_half,
                            pltpu.roll(xg, LANES - HEAD_DIM // 2, 1),
                            pltpu.roll(xg, HEAD_DIM // 2, 1))
        if g < n_groups:
            q_ref[:, g * LANES:(g + 1) * LANES] = (xg * cos_q + swapped * sin_q).astype(BF16)
        else:
            c0 = (g - n_groups) * LANES
            k_ref[:, c0:c0 + LANES] = (xg * cos + swapped * sin_signed).astype(BF16)


def _qkv_proj(x2, pos2, g, w_qkv, *, seq):
    t, d = x2.shape
    rows = QKV_ROWS
    blocks_per_seq = seq // rows
    inv_freq = ROPE_THETA ** (-jnp.arange(0, HEAD_DIM, 2, dtype=F32) / HEAD_DIM)
    invf = jnp.tile(inv_freq, LANES // (HEAD_DIM // 2)).reshape(1, LANES)
    row_spec = pl.BlockSpec((rows, d), lambda i: (i, 0))
    vt_spec = pl.BlockSpec((d, rows), lambda i: (i // blocks_per_seq, i % blocks_per_seq))
    const = lambda shape: pl.BlockSpec(shape, lambda i: (0, 0))
    w_b = w_qkv.astype(BF16)
    return pl.pallas_call(
        _qkv_kernel,
        grid=(t // rows,),
        in_specs=[row_spec, pl.BlockSpec((rows, 1), lambda i: (i, 0)),
                  const((1, d)), const((1, LANES)), const((d, 2 * d)), const((d, d))],
        out_specs=[row_spec, row_spec, vt_spec],
        out_shape=[jax.ShapeDtypeStruct((t, d), BF16), jax.ShapeDtypeStruct((t, d), BF16),
                   jax.ShapeDtypeStruct((t // seq * d, seq), BF16)],
        compiler_params=pltpu.CompilerParams(
            dimension_semantics=("arbitrary",), vmem_limit_bytes=VMEM_LIMIT_BYTES),
        name="qkv_rope",
    )(x2, pos2, g.reshape(1, d), invf, w_b[:, :2 * d], w_b[:, 2 * d:].T)


def _attn_kernel(q_ref, k_ref, vt_ref, x_ref, wo_ref, sg_ref, lq1_ref, lk1_ref,
                 lq2_ref, lk2_ref, o_ref, heads_ref, s0_ref, s1_ref, e0_ref, e1_ref,
                 *, lambda_init):
    tq = q_ref.shape[0]
    s_refs = (s0_ref, s1_ref)
    e_refs = (e0_ref, e1_ref)
    lam = (jnp.exp(jnp.sum(lq1_ref[...] * lk1_ref[...], axis=1, keepdims=True))
           - jnp.exp(jnp.sum(lq2_ref[...] * lk2_ref[...], axis=1, keepdims=True))
           + lambda_init)
    lane = lax.broadcasted_iota(jnp.int32, (1, LANES), 1)
    map1 = lane < HEAD_DIM
    sub_g = sg_ref[...] * (1.0 - lambda_init)

    def head_cols(h):
        return slice(h * V_DIM, (h + 1) * V_DIM)

    def scores(h):
        qh = q_ref[:, head_cols(h)]
        zero = jnp.zeros_like(qh)
        q_both = jnp.concatenate([jnp.where(map1, qh, zero), jnp.where(map1, zero, qh)], axis=0)
        s_refs[h % 2][...] = lax.dot_general(
            k_ref[:, head_cols(h)], q_both, (((1,), (1,)), ((), ())),
            preferred_element_type=F32)

    def softmax(h):
        s = s_refs[h % 2][...]
        e = jnp.exp2(s - jnp.max(s, axis=0, keepdims=True))
        e_refs[h % 2][...] = e.astype(BF16)
        return 1.0 / jnp.sum(e, axis=0, keepdims=True)

    def values(h, inv):
        pv = jnp.dot(vt_ref[head_cols(h), :], e_refs[h % 2][...],
                     preferred_element_type=F32)
        o_t = pv[:, :tq] * inv[:, :tq] - pv[:, tq:] * (lam * inv[:, tq:])
        heads_ref[:, head_cols(h)] = _rms(o_t.T, sub_g).astype(BF16)

    scores(0)
    inv_prev = None
    for t in range(N_HEADS):
        if t + 1 < N_HEADS:
            scores(t + 1)
        inv = softmax(t)
        if t >= 1:
            values(t - 1, inv_prev)
        inv_prev = inv
    values(N_HEADS - 1, inv_prev)

    o_ref[...] = x_ref[...] + jnp.dot(heads_ref[...], wo_ref[...], preferred_element_type=F32)


def _attention(x2, q, k, v_t, w_o, sub_g, lq1, lk1, lq2, lk2, *, seq, lambda_init):
    t, d = x2.shape
    tq = ATTN_Q_ROWS
    nq = seq // tq
    q_spec = pl.BlockSpec((tq, d), lambda b, i: (b * nq + i, 0))
    k_spec = pl.BlockSpec((seq, d), lambda b, i: (b, 0))
    vt_spec = pl.BlockSpec((d, seq), lambda b, i: (b, 0))
    vec = lambda n: pl.BlockSpec((1, n), lambda b, i: (0, 0))
    return pl.pallas_call(
        functools.partial(_attn_kernel, lambda_init=lambda_init),
        grid=(t // seq, nq),
        in_specs=[q_spec, k_spec, vt_spec, q_spec,
                  pl.BlockSpec((d, d), lambda b, i: (0, 0)),
                  vec(V_DIM), vec(HEAD_DIM), vec(HEAD_DIM), vec(HEAD_DIM), vec(HEAD_DIM)],
        out_specs=q_spec,
        out_shape=jax.ShapeDtypeStruct((t, d), F32),
        scratch_shapes=[pltpu.VMEM((tq, d), BF16),
                        pltpu.VMEM((seq, 2 * tq), F32), pltpu.VMEM((seq, 2 * tq), F32),
                        pltpu.VMEM((seq, 2 * tq), BF16), pltpu.VMEM((seq, 2 * tq), BF16)],
        compiler_params=pltpu.CompilerParams(
            dimension_semantics=("arbitrary", "arbitrary"), vmem_limit_bytes=VMEM_LIMIT_BYTES),
        name="diff_attn",
    )(q, k, v_t, x2, w_o.astype(BF16), sub_g.reshape(1, V_DIM),
      lq1.reshape(1, HEAD_DIM), lk1.reshape(1, HEAD_DIM),
      lq2.reshape(1, HEAD_DIM), lk2.reshape(1, HEAD_DIM))


def _gated_block_kernel(*refs, n_up, n_col, mixer, final_norm, blocks_per_seq, down_group):
    x_ref, xprev_ref, xnext_ref, g_ref, wup_ref, cw_ref = refs[:6]
    pos = 6
    if not mixer:
        cb_ref = refs[pos]
        pos += 1
    wd_ref = refs[pos]
    pos += 1
    if final_norm:
        fg_ref = refs[pos]
        pos += 1
    o_ref, h_ref = refs[pos:pos + 2]
    u_refs = refs[pos + 2:pos + 4]
    z_refs = refs[pos + 4:pos + 8]

    rows = x_ref.shape[0]
    cols = wd_ref.shape[1]

    n_chunks = rows // CONV_CHUNK
    g = g_ref[...]
    r = pl.program_id(0) % blocks_per_seq
    h_prev = jnp.where(r == 0, 0.0, _rms(xprev_ref[...], g))
    h_next = jnp.where(r == blocks_per_seq - 1, 0.0, _rms(xnext_ref[...], g))
    h_ref[0:HALO, :] = h_prev.astype(BF16)
    h_ref[HALO + rows:, :] = h_next.astype(BF16)
    for c in range(rows // NORM_ROWS):
        x = x_ref[c * NORM_ROWS:(c + 1) * NORM_ROWS, :]
        h_ref[HALO + c * NORM_ROWS:HALO + (c + 1) * NORM_ROWS, :] = _rms(x, g).astype(BF16)
        o_ref[c * NORM_ROWS:(c + 1) * NORM_ROWS, :] = x

    def h_span(c):
        lo = 0 if c == 0 else HALO + c * CONV_CHUNK
        hi = rows + 2 * HALO if c == n_chunks - 1 else HALO + (c + 1) * CONV_CHUNK
        return lo, hi

    def up(j, by_chunk=False):
        spans = [h_span(c) for c in range(n_chunks)] if by_chunk else [(0, rows + 2 * HALO)]
        for lo, hi in spans:
            h = h_ref[lo:hi, :]
            for n in range(n_up):
                u_refs[j % 2][lo:hi, n * cols:(n + 1) * cols] = jnp.dot(
                    h, wup_ref[n * n_col + j], preferred_element_type=F32)

    def act(j):
        u = u_refs[j % 2]

        def window(part, r0):
            return u[r0 - SUBLANES:r0 + ACT_ROWS + SUBLANES, part * cols:(part + 1) * cols]

        def conv(win, part):
            n = win.shape[0]
            cw = cw_ref[part * n_col + j]
            inner = slice(SUBLANES, SUBLANES + ACT_ROWS)
            return (cw[0:1] * pltpu.roll(win, 1, 0)[inner] + cw[1:2] * win[inner]
                    + cw[2:3] * pltpu.roll(win, n - 1, 0)[inner])

        for c in range(rows // ACT_ROWS):
            r0 = HALO + c * ACT_ROWS
            if mixer:
                gate = u[r0:r0 + ACT_ROWS, 0:cols]
                z = gate * conv(window(1, r0) * window(2, r0), 0)
            else:
                yg = conv(window(0, r0), 0) + cb_ref[j]
                yv = conv(window(1, r0), 1) + cb_ref[n_col + j]
                z = yg * jax.nn.sigmoid(yg) * yv
            z_refs[j % len(z_refs)][c * ACT_ROWS:(c + 1) * ACT_ROWS, :] = z.astype(BF16)

    def down(tiles, row_chunk=None, last=False):
        step = rows if row_chunk is None else row_chunk
        for r0 in range(0, rows, step):
            rs = slice(r0, r0 + step)
            acc = o_ref[rs, :]
            for t in tiles:
                acc = acc + jnp.dot(z_refs[t % len(z_refs)][rs, :], wd_ref[t],
                                    preferred_element_type=F32)
            o_ref[rs, :] = acc
            if last and final_norm:
                for p in range(r0, r0 + step, NORM_ROWS):
                    o_ref[p:p + NORM_ROWS, :] = _rms(o_ref[p:p + NORM_ROWS, :], fg_ref[...])

    up(0, by_chunk=True)
    done = 0
    for s in range(n_col):
        if s + 1 < n_col:
            up(s + 1)
        act(s)
        if s >= down_group and s % down_group == 0:
            down(list(range(s - down_group, s)))
            done = s
    down(list(range(done, n_col)), row_chunk=CONV_CHUNK, last=True)


def _gated_block(x2, norm_g, w_up, conv_w, conv_b, w_down, final_g, *, seq, mixer):
    t, d = x2.shape
    hidden = w_down.shape[0]
    n_up = w_up.shape[1] // hidden
    n_conv = conv_w.shape[1] // hidden
    rows, cols = BLOCK_ROWS, BLOCK_COLS
    n_col = hidden // cols
    blocks_per_seq = seq // rows
    halo_per_block = rows // HALO
    last_halo = t // HALO - 1
    final_norm = final_g is not None

    row_spec = pl.BlockSpec((rows, d), lambda i: (i, 0))
    prev_spec = pl.BlockSpec((HALO, d), lambda i: (jnp.maximum(i * halo_per_block - 1, 0), 0))
    next_spec = pl.BlockSpec((HALO, d), lambda i: (jnp.minimum((i + 1) * halo_per_block, last_halo), 0))

    def resident(a):
        zeros = (0,) * a.ndim
        return pl.BlockSpec(a.shape, lambda i: zeros, pipeline_mode=pl.Buffered(1))

    w_up_t = w_up.astype(BF16).reshape(d, n_up * n_col, cols).transpose(1, 0, 2)
    conv_w_t = conv_w.reshape(3, n_conv * n_col, cols).transpose(1, 0, 2)
    w_down_t = w_down.astype(BF16).reshape(n_col, cols, d)
    weights = [norm_g.reshape(1, d), w_up_t, conv_w_t]
    if not mixer:
        weights.append(conv_b.reshape(n_conv * n_col, 1, cols))
    weights.append(w_down_t)
    if final_norm:
        weights.append(final_g.reshape(1, d))

    return pl.pallas_call(
        functools.partial(_gated_block_kernel, n_up=n_up, n_col=n_col, mixer=mixer,
                          final_norm=final_norm, blocks_per_seq=blocks_per_seq,
                          down_group=1 if mixer else 2),
        grid=(t // rows,),
        in_specs=[row_spec, prev_spec, next_spec] + [resident(w) for w in weights],
        out_specs=row_spec,
        out_shape=jax.ShapeDtypeStruct((t, d), F32),
        scratch_shapes=[
            pltpu.VMEM((rows + 2 * HALO, d), BF16),
            pltpu.VMEM((rows + 2 * HALO, n_up * cols), F32),
            pltpu.VMEM((rows + 2 * HALO, n_up * cols), F32),
        ] + [pltpu.VMEM((rows, cols), BF16)] * 4,
        compiler_params=pltpu.CompilerParams(
            dimension_semantics=("arbitrary",), vmem_limit_bytes=VMEM_LIMIT_BYTES),
        name="mixer_block" if mixer else "conv_ffn",
    )(x2, x2, x2, *weights)


def kernel(x, positions, attn_norm_g, attn_w_qkv, attn_lambda_q1, attn_lambda_k1, attn_lambda_q2, attn_lambda_k2, attn_subln_g, attn_w_o, conv_norm_g, conv_w_in, conv_w, conv_w_out, ffn_norm_g, ffn_w_up, ffn_conv_w, ffn_conv_b, ffn_w_down, final_norm_g):
    batch, seq, d = x.shape
    depth = ffn_norm_g.shape[0]
    assert d == N_HEADS * V_DIM
    x2 = x.reshape(batch * seq, d)
    pos2 = positions.reshape(batch * seq, 1)
    for i in range(depth):
        if i % N_MIXERS == 0:
            a = i // N_MIXERS
            lambda_init = 0.8 - 0.6 * math.exp(-0.3 * i)
            q, k, v_t = _qkv_proj(x2, pos2, attn_norm_g[a], attn_w_qkv[a], seq=seq)
            x2 = _attention(x2, q, k, v_t, attn_w_o[a], attn_subln_g[a],
                            attn_lambda_q1[a], attn_lambda_k1[a],
                            attn_lambda_q2[a], attn_lambda_k2[a],
                            seq=seq, lambda_init=lambda_init)
        else:
            c = i // N_MIXERS
            x2 = _gated_block(x2, conv_norm_g[c], conv_w_in[c], conv_w[c], None,
                              conv_w_out[c], None, seq=seq, mixer=True)
        x2 = _gated_block(x2, ffn_norm_g[i], ffn_w_up[i], ffn_conv_w[i], ffn_conv_b[i],
                          ffn_w_down[i], final_norm_g if i == depth - 1 else None,
                          seq=seq, mixer=False)
    return x2.reshape(batch, seq, d)
```

```python
import functools
import math

import jax
import jax.numpy as jnp
from jax import lax
from jax.experimental import pallas as pl
from jax.experimental.pallas import tpu as pltpu

N_HEADS = 8
HEAD_DIM = 64
V_DIM = 2 * HEAD_DIM
ROPE_THETA = 10000.0
RMS_EPS = 1e-5
N_MIXERS = 2

F32 = jnp.float32
BF16 = jnp.bfloat16

LANES = 128
SUBLANES = 8
BF16_SUBLANES = 16
VMEM_LIMIT_BYTES = 56 * 1024 * 1024

QKV_ROWS = 512
ATTN_Q_ROWS = 512
ATTN_UNIT_ROWS = 256
BLOCK_ROWS = 1024
BLOCK_COLS = 256
CONV_CHUNK = 256
ACT_ROWS = 256
NORM_ROWS = 32
HALO = BF16_SUBLANES


def _rms(x, g):
    return x * lax.rsqrt(jnp.mean(x * x, axis=-1, keepdims=True) + RMS_EPS) * g


def _qkv_kernel(x_ref, pos_ref, g_ref, invf_ref, w_ref, q_ref, k_ref, v_ref):
    d = x_ref.shape[1]
    h = _rms(x_ref[...], g_ref[...]).astype(BF16)
    acc = jnp.dot(h, w_ref[...], preferred_element_type=F32)

    ang = pos_ref[...].astype(F32) * invf_ref[...]
    cos = jnp.cos(ang)
    sin = jnp.sin(ang)
    lane = lax.broadcasted_iota(jnp.int32, (1, LANES), 1)
    first_half = (lane & (HEAD_DIM // 2)) == 0
    sin_signed = jnp.where(first_half, -sin, sin)
    q_scale = HEAD_DIM ** -0.5 * math.log2(math.e)
    cos_q, sin_q = cos * q_scale, sin_signed * q_scale

    n_groups = d // LANES
    for g in range(2 * n_groups):
        xg = acc[:, g * LANES:(g + 1) * LANES]
        swapped = jnp.where(first_half,
                            pltpu.roll(xg, LANES - HEAD_DIM // 2, 1),
                            pltpu.roll(xg, HEAD_DIM // 2, 1))
        if g < n_groups:
            q_ref[:, g * LANES:(g + 1) * LANES] = (xg * cos_q + swapped * sin_q).astype(BF16)
        else:
            c0 = (g - n_groups) * LANES
            k_ref[:, c0:c0 + LANES] = (xg * cos + swapped * sin_signed).astype(BF16)
    v_ref[...] = acc[:, 2 * d:].astype(BF16)


def _qkv_proj(x2, pos2, g, w_qkv):
    t, d = x2.shape
    rows = QKV_ROWS
    inv_freq = ROPE_THETA ** (-jnp.arange(0, HEAD_DIM, 2, dtype=F32) / HEAD_DIM)
    invf = jnp.tile(inv_freq, LANES // (HEAD_DIM // 2)).reshape(1, LANES)
    out = jax.ShapeDtypeStruct((t, d), BF16)
    row_spec = pl.BlockSpec((rows, d), lambda i: (i, 0))
    return pl.pallas_call(
        _qkv_kernel,
        grid=(t // rows,),
        in_specs=[
            row_spec,
            pl.BlockSpec((rows, 1), lambda i: (i, 0)),
            pl.BlockSpec((1, d), lambda i: (0, 0)),
            pl.BlockSpec((1, LANES), lambda i: (0, 0)),
            pl.BlockSpec((d, 3 * d), lambda i: (0, 0)),
        ],
        out_specs=[row_spec, row_spec, row_spec],
        out_shape=[out, out, out],
        compiler_params=pltpu.CompilerParams(
            dimension_semantics=("arbitrary",), vmem_limit_bytes=VMEM_LIMIT_BYTES),
        name="qkv_rope",
    )(x2, pos2, g.reshape(1, d), invf, w_qkv.astype(BF16))


def _attn_kernel(q_ref, k_ref, v_ref, x_ref, wo_ref, sg_ref, lq1_ref, lk1_ref,
                 lq2_ref, lk2_ref, o_ref, heads_ref, s0_ref, s1_ref, e0_ref, e1_ref,
                 *, lambda_init):
    tq = ATTN_UNIT_ROWS
    units = [(h, r) for h in range(N_HEADS) for r in range(q_ref.shape[0] // tq)]
    s_refs = (s0_ref, s1_ref)
    e_refs = (e0_ref, e1_ref)
    lam = (jnp.exp(jnp.sum(lq1_ref[...] * lk1_ref[...], axis=1, keepdims=True))
           - jnp.exp(jnp.sum(lq2_ref[...] * lk2_ref[...], axis=1, keepdims=True))
           + lambda_init)
    lane = lax.broadcasted_iota(jnp.int32, (1, LANES), 1)
    map1 = lane < HEAD_DIM
    sub_g = sg_ref[...] * (1.0 - lambda_init)

    def head_cols(h):
        return slice(h * V_DIM, (h + 1) * V_DIM)

    def scores(u):
        h, r = units[u]
        qh = q_ref[r * tq:(r + 1) * tq, head_cols(h)]
        zero = jnp.zeros_like(qh)
        q_both = jnp.concatenate([jnp.where(map1, qh, zero), jnp.where(map1, zero, qh)], axis=0)
        s_refs[u % 2][...] = lax.dot_general(
            q_both, k_ref[:, head_cols(h)], (((1,), (1,)), ((), ())),
            preferred_element_type=F32)

    def softmax(u):
        s = s_refs[u % 2][...]
        shifted = (s - jnp.max(s, axis=1, keepdims=True)).astype(BF16)
        e_refs[u % 2][...] = jnp.exp2(shifted)

    ones = jnp.ones((k_ref.shape[0], V_DIM), BF16)

    def values(u):
        h, r = units[u]
        v_ext = jnp.concatenate([v_ref[:, head_cols(h)], ones], axis=1)
        pv = jnp.dot(e_refs[u % 2][...], v_ext, preferred_element_type=F32)
        p = pv[:, :V_DIM] / pv[:, V_DIM:]
        o = p[:tq] - lam * p[tq:]
        heads_ref[r * tq:(r + 1) * tq, head_cols(h)] = _rms(o, sub_g).astype(BF16)

    scores(0)
    for t in range(len(units)):
        if t + 1 < len(units):
            scores(t + 1)
        softmax(t)
        if t >= 1:
            values(t - 1)
    values(len(units) - 1)

    o_ref[...] = x_ref[...] + jnp.dot(heads_ref[...], wo_ref[...], preferred_element_type=F32)


def _attention(x2, q, k, v, w_o, sub_g, lq1, lk1, lq2, lk2, *, seq, lambda_init):
    t, d = x2.shape
    rows, tq = ATTN_Q_ROWS, ATTN_UNIT_ROWS
    nq = seq // rows
    q_spec = pl.BlockSpec((rows, d), lambda b, i: (b * nq + i, 0))
    kv_spec = pl.BlockSpec((seq, d), lambda b, i: (b, 0))
    vec = lambda n: pl.BlockSpec((1, n), lambda b, i: (0, 0))
    return pl.pallas_call(
        functools.partial(_attn_kernel, lambda_init=lambda_init),
        grid=(t // seq, nq),
        in_specs=[q_spec, kv_spec, kv_spec, q_spec,
                  pl.BlockSpec((d, d), lambda b, i: (0, 0)),
                  vec(V_DIM), vec(HEAD_DIM), vec(HEAD_DIM), vec(HEAD_DIM), vec(HEAD_DIM)],
        out_specs=q_spec,
        out_shape=jax.ShapeDtypeStruct((t, d), F32),
        scratch_shapes=[pltpu.VMEM((rows, d), BF16),
                        pltpu.VMEM((2 * tq, seq), F32), pltpu.VMEM((2 * tq, seq), F32),
                        pltpu.VMEM((2 * tq, seq), BF16), pltpu.VMEM((2 * tq, seq), BF16)],
        compiler_params=pltpu.CompilerParams(
            dimension_semantics=("arbitrary", "arbitrary"), vmem_limit_bytes=VMEM_LIMIT_BYTES),
        name="diff_attn",
    )(q, k, v, x2, w_o.astype(BF16), sub_g.reshape(1, V_DIM),
      lq1.reshape(1, HEAD_DIM), lk1.reshape(1, HEAD_DIM),
      lq2.reshape(1, HEAD_DIM), lk2.reshape(1, HEAD_DIM))


def _gated_block_kernel(*refs, n_up, n_col, mixer, final_norm, blocks_per_seq, down_group):
    x_ref, xprev_ref, xnext_ref, g_ref, wup_ref, cw_ref = refs[:6]
    pos = 6
    if not mixer:
        cb_ref = refs[pos]
        pos += 1
    wd_ref = refs[pos]
    pos += 1
    if final_norm:
        fg_ref = refs[pos]
        pos += 1
    o_ref, h_ref = refs[pos:pos + 2]
    u_refs = refs[pos + 2:pos + 4]
    z_refs = refs[pos + 4:pos + 8]

    rows = x_ref.shape[0]
    cols = wd_ref.shape[1]

    n_chunks = rows // CONV_CHUNK
    g = g_ref[...]
    r = pl.program_id(0) % blocks_per_seq
    h_prev = jnp.where(r == 0, 0.0, _rms(xprev_ref[...], g))
    h_next = jnp.where(r == blocks_per_seq - 1, 0.0, _rms(xnext_ref[...], g))
    h_ref[0:HALO, :] = h_prev.astype(BF16)
    h_ref[HALO + rows:, :] = h_next.astype(BF16)
    for c in range(rows // NORM_ROWS):
        x = x_ref[c * NORM_ROWS:(c + 1) * NORM_ROWS, :]
        h_ref[HALO + c * NORM_ROWS:HALO + (c + 1) * NORM_ROWS, :] = _rms(x, g).astype(BF16)
        o_ref[c * NORM_ROWS:(c + 1) * NORM_ROWS, :] = x

    def h_span(c):
        lo = 0 if c == 0 else HALO + c * CONV_CHUNK
        hi = rows + 2 * HALO if c == n_chunks - 1 else HALO + (c + 1) * CONV_CHUNK
        return lo, hi

    def up(j, by_chunk=False):
        spans = [h_span(c) for c in range(n_chunks)] if by_chunk else [(0, rows + 2 * HALO)]
        for lo, hi in spans:
            h = h_ref[lo:hi, :]
            for n in range(n_up):
                u_refs[j % 2][lo:hi, n * cols:(n + 1) * cols] = jnp.dot(
                    h, wup_ref[n * n_col + j], preferred_element_type=F32)

    def act(j):
        u = u_refs[j % 2]

        def window(part, r0):
            return u[r0 - SUBLANES:r0 + ACT_ROWS + SUBLANES, part * cols:(part + 1) * cols]

        def conv(win, part):
            n = win.shape[0]
            cw = cw_ref[part * n_col + j]
            inner = slice(SUBLANES, SUBLANES + ACT_ROWS)
            return (cw[0:1] * pltpu.roll(win, 1, 0)[inner] + cw[1:2] * win[inner]
                    + cw[2:3] * pltpu.roll(win, n - 1, 0)[inner])

        for c in range(rows // ACT_ROWS):
            r0 = HALO + c * ACT_ROWS
            if mixer:
                gate = u[r0:r0 + ACT_ROWS, 0:cols]
                z = gate * conv(window(1, r0) * window(2, r0), 0)
            else:
                yg = conv(window(0, r0), 0) + cb_ref[j]
                yv = conv(window(1, r0), 1) + cb_ref[n_col + j]
                z = yg * jax.nn.sigmoid(yg) * yv
            z_refs[j % len(z_refs)][c * ACT_ROWS:(c + 1) * ACT_ROWS, :] = z.astype(BF16)

    def down(tiles, row_chunk=None, last=False):
        step = rows if row_chunk is None else row_chunk
        for r0 in range(0, rows, step):
            rs = slice(r0, r0 + step)
            acc = o_ref[rs, :]
            for t in tiles:
                acc = acc + jnp.dot(z_refs[t % len(z_refs)][rs, :], wd_ref[t],
                                    preferred_element_type=F32)
            o_ref[rs, :] = acc
            if last and final_norm:
                for p in range(r0, r0 + step, NORM_ROWS):
                    o_ref[p:p + NORM_ROWS, :] = _rms(o_ref[p:p + NORM_ROWS, :], fg_ref[...])

    up(0, by_chunk=True)
    done = 0
    for s in range(n_col):
        if s + 1 < n_col:
            up(s + 1)
        act(s)
        if s >= down_group and s % down_group == 0:
            down(list(range(s - down_group, s)))
            done = s
    down(list(range(done, n_col)), row_chunk=CONV_CHUNK, last=True)


def _gated_block(x2, norm_g, w_up, conv_w, conv_b, w_down, final_g, *, seq, mixer):
    t, d = x2.shape
    hidden = w_down.shape[0]
    n_up = w_up.shape[1] // hidden
    n_conv = conv_w.shape[1] // hidden
    rows, cols = BLOCK_ROWS, BLOCK_COLS
    n_col = hidden // cols
    blocks_per_seq = seq // rows
    halo_per_block = rows // HALO
    last_halo = t // HALO - 1
    final_norm = final_g is not None

    row_spec = pl.BlockSpec((rows, d), lambda i: (i, 0))
    prev_spec = pl.BlockSpec((HALO, d), lambda i: (jnp.maximum(i * halo_per_block - 1, 0), 0))
    next_spec = pl.BlockSpec((HALO, d), lambda i: (jnp.minimum((i + 1) * halo_per_block, last_halo), 0))

    def resident(a):
        zeros = (0,) * a.ndim
        return pl.BlockSpec(a.shape, lambda i: zeros, pipeline_mode=pl.Buffered(1))

    w_up_t = w_up.astype(BF16).reshape(d, n_up * n_col, cols).transpose(1, 0, 2)
    conv_w_t = conv_w.reshape(3, n_conv * n_col, cols).transpose(1, 0, 2)
    w_down_t = w_down.astype(BF16).reshape(n_col, cols, d)
    weights = [norm_g.reshape(1, d), w_up_t, conv_w_t]
    if not mixer:
        weights.append(conv_b.reshape(n_conv * n_col, 1, cols))
    weights.append(w_down_t)
    if final_norm:
        weights.append(final_g.reshape(1, d))

    return pl.pallas_call(
        functools.partial(_gated_block_kernel, n_up=n_up, n_col=n_col, mixer=mixer,
                          final_norm=final_norm, blocks_per_seq=blocks_per_seq,
                          down_group=1 if mixer else 2),
        grid=(t // rows,),
        in_specs=[row_spec, prev_spec, next_spec] + [resident(w) for w in weights],
        out_specs=row_spec,
        out_shape=jax.ShapeDtypeStruct((t, d), F32),
        scratch_shapes=[
            pltpu.VMEM((rows + 2 * HALO, d), BF16),
            pltpu.VMEM((rows + 2 * HALO, n_up * cols), F32),
            pltpu.VMEM((rows + 2 * HALO, n_up * cols), F32),
        ] + [pltpu.VMEM((rows, cols), BF16)] * 4,
        compiler_params=pltpu.CompilerParams(
            dimension_semantics=("arbitrary",), vmem_limit_bytes=VMEM_LIMIT_BYTES),
        name="mixer_block" if mixer else "conv_ffn",
    )(x2, x2, x2, *weights)


def kernel(x, positions, attn_norm_g, attn_w_qkv, attn_lambda_q1, attn_lambda_k1, attn_lambda_q2, attn_lambda_k2, attn_subln_g, attn_w_o, conv_norm_g, conv_w_in, conv_w, conv_w_out, ffn_norm_g, ffn_w_up, ffn_conv_w, ffn_conv_b, ffn_w_down, final_norm_g):
    batch, seq, d = x.shape
    depth = ffn_norm_g.shape[0]
    assert d == N_HEADS * V_DIM
    x2 = x.reshape(batch * seq, d)
    pos2 = positions.reshape(batch * seq, 1)
    for i in range(depth):
        if i % N_MIXERS == 0:
            a = i // N_MIXERS
            lambda_init = 0.8 - 0.6 * math.exp(-0.3 * i)
            q, k, v = _qkv_proj(x2, pos2, attn_norm_g[a], attn_w_qkv[a])
            x2 = _attention(x2, q, k, v, attn_w_o[a], attn_subln_g[a],
                            attn_lambda_q1[a], attn_lambda_k1[a],
                            attn_lambda_q2[a], attn_lambda_k2[a],
                            seq=seq, lambda_init=lambda_init)
        else:
            c = i // N_MIXERS
            x2 = _gated_block(x2, conv_norm_g[c], conv_w_in[c], conv_w[c], None,
                              conv_w_out[c], None, seq=seq, mixer=True)
        x2 = _gated_block(x2, ffn_norm_g[i], ffn_w_up[i], ffn_conv_w[i], ffn_conv_b[i],
                          ffn_w_down[i], final_norm_g if i == depth - 1 else None,
                          seq=seq, mixer=False)
    return x2.reshape(batch, seq, d)
```

```python
import functools
import math

import jax
import jax.numpy as jnp
from jax import lax
from jax.experimental import pallas as pl
from jax.experimental.pallas import tpu as pltpu

N_HEADS = 8
HEAD_DIM = 64
V_DIM = 2 * HEAD_DIM
ROPE_THETA = 10000.0
RMS_EPS = 1e-5
N_MIXERS = 2

F32 = jnp.float32
BF16 = jnp.bfloat16

LANES = 128
SUBLANES = 8
BF16_SUBLANES = 16
VMEM_LIMIT_BYTES = 56 * 1024 * 1024

QKV_ROWS = 512
ATTN_Q_ROWS = 512
ATTN_UNIT_ROWS = 256
BLOCK_ROWS = 1024
BLOCK_COLS = 256
CONV_CHUNK = 256
ACT_ROWS = 512
NORM_ROWS = 32
HALO = BF16_SUBLANES


def _rms(x, g):
    return x * lax.rsqrt(jnp.mean(x * x, axis=-1, keepdims=True) + RMS_EPS) * g


def _qkv_kernel(x_ref, pos_ref, g_ref, invf_ref, w_ref, q_ref, k_ref, v_ref):
    d = x_ref.shape[1]
    h = _rms(x_ref[...], g_ref[...]).astype(BF16)
    acc = jnp.dot(h, w_ref[...], preferred_element_type=F32)

    ang = pos_ref[...].astype(F32) * invf_ref[...]
    cos = jnp.cos(ang)
    sin = jnp.sin(ang)
    lane = lax.broadcasted_iota(jnp.int32, (1, LANES), 1)
    first_half = (lane & (HEAD_DIM // 2)) == 0
    sin_signed = jnp.where(first_half, -sin, sin)
    q_scale = HEAD_DIM ** -0.5 * math.log2(math.e)
    cos_q, sin_q = cos * q_scale, sin_signed * q_scale

    n_groups = d // LANES
    for g in range(2 * n_groups):
        xg = acc[:, g * LANES:(g + 1) * LANES]
        swapped = jnp.where(first_half,
                            pltpu.roll(xg, LANES - HEAD_DIM // 2, 1),
                            pltpu.roll(xg, HEAD_DIM // 2, 1))
        if g < n_groups:
            q_ref[:, g * LANES:(g + 1) * LANES] = (xg * cos_q + swapped * sin_q).astype(BF16)
        else:
            c0 = (g - n_groups) * LANES
            k_ref[:, c0:c0 + LANES] = (xg * cos + swapped * sin_signed).astype(BF16)
    v_ref[...] = acc[:, 2 * d:].astype(BF16)


def _qkv_proj(x2, pos2, g, w_qkv):
    t, d = x2.shape
    rows = QKV_ROWS
    inv_freq = ROPE_THETA ** (-jnp.arange(0, HEAD_DIM, 2, dtype=F32) / HEAD_DIM)
    invf = jnp.tile(inv_freq, LANES // (HEAD_DIM // 2)).reshape(1, LANES)
    out = jax.ShapeDtypeStruct((t, d), BF16)
    row_spec = pl.BlockSpec((rows, d), lambda i: (i, 0))
    return pl.pallas_call(
        _qkv_kernel,
        grid=(t // rows,),
        in_specs=[
            row_spec,
            pl.BlockSpec((rows, 1), lambda i: (i, 0)),
            pl.BlockSpec((1, d), lambda i: (0, 0)),
            pl.BlockSpec((1, LANES), lambda i: (0, 0)),
            pl.BlockSpec((d, 3 * d), lambda i: (0, 0)),
        ],
        out_specs=[row_spec, row_spec, row_spec],
        out_shape=[out, out, out],
        compiler_params=pltpu.CompilerParams(
            dimension_semantics=("arbitrary",), vmem_limit_bytes=VMEM_LIMIT_BYTES),
        name="qkv_rope",
    )(x2, pos2, g.reshape(1, d), invf, w_qkv.astype(BF16))


def _attn_kernel(q_ref, k_ref, v_ref, x_ref, wo_ref, sg_ref, lq1_ref, lk1_ref,
                 lq2_ref, lk2_ref, o_ref, heads_ref, s0_ref, s1_ref, e0_ref, e1_ref,
                 *, lambda_init):
    tq = ATTN_UNIT_ROWS
    units = [(h, r) for h in range(N_HEADS) for r in range(q_ref.shape[0] // tq)]
    s_refs = (s0_ref, s1_ref)
    e_refs = (e0_ref, e1_ref)
    lam = (jnp.exp(jnp.sum(lq1_ref[...] * lk1_ref[...], axis=1, keepdims=True))
           - jnp.exp(jnp.sum(lq2_ref[...] * lk2_ref[...], axis=1, keepdims=True))
           + lambda_init)
    lane = lax.broadcasted_iota(jnp.int32, (1, LANES), 1)
    map1 = lane < HEAD_DIM
    sub_g = sg_ref[...] * (1.0 - lambda_init)

    def head_cols(h):
        return slice(h * V_DIM, (h + 1) * V_DIM)

    def scores(u):
        h, r = units[u]
        qh = q_ref[r * tq:(r + 1) * tq, head_cols(h)]
        zero = jnp.zeros_like(qh)
        q_both = jnp.concatenate([jnp.where(map1, qh, zero), jnp.where(map1, zero, qh)], axis=0)
        s_refs[u % 2][...] = lax.dot_general(
            q_both, k_ref[:, head_cols(h)], (((1,), (1,)), ((), ())),
            preferred_element_type=F32)

    def softmax(u):
        s = s_refs[u % 2][...]
        shifted = (s - jnp.max(s, axis=1, keepdims=True)).astype(BF16)
        e_refs[u % 2][...] = jnp.exp2(shifted)

    ones = jnp.ones((k_ref.shape[0], V_DIM), BF16)

    def values(u):
        h, r = units[u]
        v_ext = jnp.concatenate([v_ref[:, head_cols(h)], ones], axis=1)
        pv = jnp.dot(e_refs[u % 2][...], v_ext, preferred_element_type=F32)
        p = pv[:, :V_DIM] / pv[:, V_DIM:]
        o = p[:tq] - lam * p[tq:]
        heads_ref[r * tq:(r + 1) * tq, head_cols(h)] = _rms(o, sub_g).astype(BF16)

    scores(0)
    for t in range(len(units)):
        if t + 1 < len(units):
            scores(t + 1)
        softmax(t)
        if t >= 1:
            values(t - 1)
    values(len(units) - 1)

    o_ref[...] = x_ref[...] + jnp.dot(heads_ref[...], wo_ref[...], preferred_element_type=F32)


def _attention(x2, q, k, v, w_o, sub_g, lq1, lk1, lq2, lk2, *, seq, lambda_init):
    t, d = x2.shape
    rows, tq = ATTN_Q_ROWS, ATTN_UNIT_ROWS
    nq = seq // rows
    q_spec = pl.BlockSpec((rows, d), lambda b, i: (b * nq + i, 0))
    kv_spec = pl.BlockSpec((seq, d), lambda b, i: (b, 0))
    vec = lambda n: pl.BlockSpec((1, n), lambda b, i: (0, 0))
    return pl.pallas_call(
        functools.partial(_attn_kernel, lambda_init=lambda_init),
        grid=(t // seq, nq),
        in_specs=[q_spec, kv_spec, kv_spec, q_spec,
                  pl.BlockSpec((d, d), lambda b, i: (0, 0)),
                  vec(V_DIM), vec(HEAD_DIM), vec(HEAD_DIM), vec(HEAD_DIM), vec(HEAD_DIM)],
        out_specs=q_spec,
        out_shape=jax.ShapeDtypeStruct((t, d), F32),
        scratch_shapes=[pltpu.VMEM((rows, d), BF16),
                        pltpu.VMEM((2 * tq, seq), F32), pltpu.VMEM((2 * tq, seq), F32),
                        pltpu.VMEM((2 * tq, seq), BF16), pltpu.VMEM((2 * tq, seq), BF16)],
        compiler_params=pltpu.CompilerParams(
            dimension_semantics=("arbitrary", "arbitrary"), vmem_limit_bytes=VMEM_LIMIT_BYTES),
        name="diff_attn",
    )(q, k, v, x2, w_o.astype(BF16), sub_g.reshape(1, V_DIM),
      lq1.reshape(1, HEAD_DIM), lk1.reshape(1, HEAD_DIM),
      lq2.reshape(1, HEAD_DIM), lk2.reshape(1, HEAD_DIM))


def _gated_block_kernel(*refs, n_up, n_col, mixer, final_norm, blocks_per_seq, down_group):
    x_ref, xprev_ref, xnext_ref, g_ref, wup_ref, cw_ref = refs[:6]
    pos = 6
    if not mixer:
        cb_ref = refs[pos]
        pos += 1
    wd_ref = refs[pos]
    pos += 1
    if final_norm:
        fg_ref = refs[pos]
        pos += 1
    o_ref, h_ref = refs[pos:pos + 2]
    u_refs = refs[pos + 2:pos + 4]
    z_refs = refs[pos + 4:pos + 6]

    rows = x_ref.shape[0]
    cols = wd_ref.shape[1]

    n_chunks = rows // CONV_CHUNK
    g = g_ref[...]
    r = pl.program_id(0) % blocks_per_seq
    h_prev = jnp.where(r == 0, 0.0, _rms(xprev_ref[...], g))
    h_next = jnp.where(r == blocks_per_seq - 1, 0.0, _rms(xnext_ref[...], g))
    h_ref[0:HALO, :] = h_prev.astype(BF16)
    h_ref[HALO + rows:, :] = h_next.astype(BF16)
    for c in range(rows // NORM_ROWS):
        x = x_ref[c * NORM_ROWS:(c + 1) * NORM_ROWS, :]
        h_ref[HALO + c * NORM_ROWS:HALO + (c + 1) * NORM_ROWS, :] = _rms(x, g).astype(BF16)
        o_ref[c * NORM_ROWS:(c + 1) * NORM_ROWS, :] = x

    def h_span(c):
        lo = 0 if c == 0 else HALO + c * CONV_CHUNK
        hi = rows + 2 * HALO if c == n_chunks - 1 else HALO + (c + 1) * CONV_CHUNK
        return lo, hi

    def up(j, by_chunk=False):
        spans = [h_span(c) for c in range(n_chunks)] if by_chunk else [(0, rows + 2 * HALO)]
        for lo, hi in spans:
            h = h_ref[lo:hi, :]
            for n in range(n_up):
                u_refs[j % 2][lo:hi, n * cols:(n + 1) * cols] = jnp.dot(
                    h, wup_ref[n * n_col + j], preferred_element_type=F32)

    def act(j):
        u = u_refs[j % 2]

        def window(part, r0):
            return u[r0 - SUBLANES:r0 + ACT_ROWS + SUBLANES, part * cols:(part + 1) * cols]

        def conv(win, part):
            n = win.shape[0]
            cw = cw_ref[part * n_col + j]
            inner = slice(SUBLANES, SUBLANES + ACT_ROWS)
            return (cw[0:1] * pltpu.roll(win, 1, 0)[inner] + cw[1:2] * win[inner]
                    + cw[2:3] * pltpu.roll(win, n - 1, 0)[inner])

        for c in range(rows // ACT_ROWS):
            r0 = HALO + c * ACT_ROWS
            if mixer:
                gate = u[r0:r0 + ACT_ROWS, 0:cols]
                z = gate * conv(window(1, r0) * window(2, r0), 0)
            else:
                yg = conv(window(0, r0), 0) + cb_ref[j]
                yv = conv(window(1, r0), 1) + cb_ref[n_col + j]
                z = yg * jax.nn.sigmoid(yg) * yv
            group, slot = divmod(j, down_group)
            z_refs[group % 2][c * ACT_ROWS:(c + 1) * ACT_ROWS,
                              slot * cols:(slot + 1) * cols] = z.astype(BF16)

    def down(tiles, row_chunk=None, last=False):
        group = tiles[0] // down_group
        k = len(tiles) * cols
        wd = wd_ref[tiles[0]:tiles[0] + len(tiles)].reshape(k, wd_ref.shape[2])
        step = rows if row_chunk is None else row_chunk
        for r0 in range(0, rows, step):
            rs = slice(r0, r0 + step)
            o_ref[rs, :] += jnp.dot(z_refs[group % 2][rs, 0:k], wd, preferred_element_type=F32)
            if last and final_norm:
                for p in range(r0, r0 + step, NORM_ROWS):
                    o_ref[p:p + NORM_ROWS, :] = _rms(o_ref[p:p + NORM_ROWS, :], fg_ref[...])

    up(0, by_chunk=True)
    done = 0
    for s in range(n_col):
        if s + 1 < n_col:
            up(s + 1)
        act(s)
        if s >= down_group and s % down_group == 0:
            down(list(range(s - down_group, s)))
            done = s
    down(list(range(done, n_col)), row_chunk=CONV_CHUNK, last=True)


def _gated_block(x2, norm_g, w_up, conv_w, conv_b, w_down, final_g, *, seq, mixer):
    t, d = x2.shape
    hidden = w_down.shape[0]
    n_up = w_up.shape[1] // hidden
    n_conv = conv_w.shape[1] // hidden
    rows, cols = BLOCK_ROWS, BLOCK_COLS
    n_col = hidden // cols
    blocks_per_seq = seq // rows
    halo_per_block = rows // HALO
    last_halo = t // HALO - 1
    final_norm = final_g is not None

    row_spec = pl.BlockSpec((rows, d), lambda i: (i, 0))
    prev_spec = pl.BlockSpec((HALO, d), lambda i: (jnp.maximum(i * halo_per_block - 1, 0), 0))
    next_spec = pl.BlockSpec((HALO, d), lambda i: (jnp.minimum((i + 1) * halo_per_block, last_halo), 0))

    def resident(a):
        zeros = (0,) * a.ndim
        return pl.BlockSpec(a.shape, lambda i: zeros, pipeline_mode=pl.Buffered(1))

    w_up_t = w_up.astype(BF16).reshape(d, n_up * n_col, cols).transpose(1, 0, 2)
    conv_w_t = conv_w.reshape(3, n_conv * n_col, cols).transpose(1, 0, 2)
    w_down_t = w_down.astype(BF16).reshape(n_col, cols, d)
    weights = [norm_g.reshape(1, d), w_up_t, conv_w_t]
    if not mixer:
        weights.append(conv_b.reshape(n_conv * n_col, 1, cols))
    weights.append(w_down_t)
    if final_norm:
        weights.append(final_g.reshape(1, d))

    down_group = 1 if mixer else 5
    return pl.pallas_call(
        functools.partial(_gated_block_kernel, n_up=n_up, n_col=n_col, mixer=mixer,
                          final_norm=final_norm, blocks_per_seq=blocks_per_seq,
                          down_group=down_group),
        grid=(t // rows,),
        in_specs=[row_spec, prev_spec, next_spec] + [resident(w) for w in weights],
        out_specs=row_spec,
        out_shape=jax.ShapeDtypeStruct((t, d), F32),
        scratch_shapes=[
            pltpu.VMEM((rows + 2 * HALO, d), BF16),
            pltpu.VMEM((rows + 2 * HALO, n_up * cols), F32),
            pltpu.VMEM((rows + 2 * HALO, n_up * cols), F32),
        ] + [pltpu.VMEM((rows, down_group * cols), BF16)] * 2,
        compiler_params=pltpu.CompilerParams(
            dimension_semantics=("arbitrary",), vmem_limit_bytes=VMEM_LIMIT_BYTES),
        name="mixer_block" if mixer else "conv_ffn",
    )(x2, x2, x2, *weights)


def kernel(x, positions, attn_norm_g, attn_w_qkv, attn_lambda_q1, attn_lambda_k1, attn_lambda_q2, attn_lambda_k2, attn_subln_g, attn_w_o, conv_norm_g, conv_w_in, conv_w, conv_w_out, ffn_norm_g, ffn_w_up, ffn_conv_w, ffn_conv_b, ffn_w_down, final_norm_g):
    batch, seq, d = x.shape
    depth = ffn_norm_g.shape[0]
    assert d == N_HEADS * V_DIM
    x2 = x.reshape(batch * seq, d)
    pos2 = positions.reshape(batch * seq, 1)
    for i in range(depth):
        if i % N_MIXERS == 0:
            a = i // N_MIXERS
            lambda_init = 0.8 - 0.6 * math.exp(-0.3 * i)
            q, k, v = _qkv_proj(x2, pos2, attn_norm_g[a], attn_w_qkv[a])
            x2 = _attention(x2, q, k, v, attn_w_o[a], attn_subln_g[a],
                            attn_lambda_q1[a], attn_lambda_k1[a],
                            attn_lambda_q2[a], attn_lambda_k2[a],
                            seq=seq, lambda_init=lambda_init)
        else:
            c = i // N_MIXERS
            x2 = _gated_block(x2, conv_norm_g[c], conv_w_in[c], conv_w[c], None,
                              conv_w_out[c], None, seq=seq, mixer=True)
        x2 = _gated_block(x2, ffn_norm_g[i], ffn_w_up[i], ffn_conv_w[i], ffn_conv_b[i],
                          ffn_w_down[i], final_norm_g if i == depth - 1 else None,
                          seq=seq, mixer=False)
    return x2.reshape(batch, seq, d)
```

```python
import functools
import math

import jax
import jax.numpy as jnp
from jax import lax
from jax.experimental import pallas as pl
from jax.experimental.pallas import tpu as pltpu

N_HEADS = 8
HEAD_DIM = 64
V_DIM = 2 * HEAD_DIM
ROPE_THETA = 10000.0
RMS_EPS = 1e-5
N_MIXERS = 2

F32 = jnp.float32
BF16 = jnp.bfloat16

LANES = 128
SUBLANES = 8
BF16_SUBLANES = 16
V7X_VMEM_BYTES = 64 * 1024 * 1024

QKV_ROWS = 512
ATTN_Q_ROWS = 512
ATTN_UNIT_ROWS = 256
BLOCK_ROWS = 1024
BLOCK_COLS = 256
CONV_CHUNK = 256
ACT_ROWS = 512
NORM_ROWS = 32
HALO = BF16_SUBLANES


def _nbytes(shape, dtype):
    return math.prod(shape) * jnp.dtype(dtype).itemsize


def _vmem_limit(pipelined, single):
    need = 2 * sum(pipelined) + sum(single)
    assert need <= V7X_VMEM_BYTES, need
    return need


def _rms(x, g):
    return x * lax.rsqrt(jnp.mean(x * x, axis=-1, keepdims=True) + RMS_EPS) * g


def _qkv_kernel(x_ref, pos_ref, g_ref, invf_ref, w_ref, q_ref, k_ref, v_ref):
    d = x_ref.shape[1]
    h = _rms(x_ref[...], g_ref[...]).astype(BF16)
    acc = jnp.dot(h, w_ref[...], preferred_element_type=F32)

    ang_t = invf_ref[...] * pos_ref[...].astype(F32)
    quarters = LANES // (HEAD_DIM // 2)
    cos = jnp.concatenate([jnp.cos(ang_t)] * quarters, axis=0).T
    sin = jnp.concatenate([jnp.sin(ang_t)] * quarters, axis=0).T
    lane = lax.broadcasted_iota(jnp.int32, (1, LANES), 1)
    first_half = (lane & (HEAD_DIM // 2)) == 0
    sin_signed = jnp.where(first_half, -sin, sin)
    q_scale = HEAD_DIM ** -0.5 * math.log2(math.e)
    cos_q, sin_q = cos * q_scale, sin_signed * q_scale

    n_groups = d // LANES
    for g in range(2 * n_groups):
        xg = acc[:, g * LANES:(g + 1) * LANES]
        swapped = jnp.where(first_half,
                            pltpu.roll(xg, LANES - HEAD_DIM // 2, 1),
                            pltpu.roll(xg, HEAD_DIM // 2, 1))
        if g < n_groups:
            q_ref[:, g * LANES:(g + 1) * LANES] = (xg * cos_q + swapped * sin_q).astype(BF16)
        else:
            c0 = (g - n_groups) * LANES
            k_ref[:, c0:c0 + LANES] = (xg * cos + swapped * sin_signed).astype(BF16)
    v_ref[...] = acc[:, 2 * d:].astype(BF16)


def _qkv_proj(x2, positions, g, w_qkv):
    t, d = x2.shape
    rows = QKV_ROWS
    inv_freq = ROPE_THETA ** (-jnp.arange(0, HEAD_DIM, 2, dtype=F32) / HEAD_DIM)
    invf = inv_freq.reshape(HEAD_DIM // 2, 1)
    pos3 = positions.reshape(t // rows, 1, rows)
    out = jax.ShapeDtypeStruct((t, d), BF16)
    row_spec = pl.BlockSpec((rows, d), lambda i: (i, 0))
    vmem = _vmem_limit(
        pipelined=[_nbytes((rows, d), F32), 3 * _nbytes((rows, d), BF16),
                   _nbytes((d, 3 * d), BF16)],
        single=[_nbytes((rows, 3 * d), F32), _nbytes((rows, d), BF16)])
    return pl.pallas_call(
        _qkv_kernel,
        grid=(t // rows,),
        in_specs=[
            row_spec,
            pl.BlockSpec((None, 1, rows), lambda i: (i, 0, 0)),
            pl.BlockSpec((1, d), lambda i: (0, 0)),
            pl.BlockSpec((HEAD_DIM // 2, 1), lambda i: (0, 0)),
            pl.BlockSpec((d, 3 * d), lambda i: (0, 0)),
        ],
        out_specs=[row_spec, row_spec, row_spec],
        out_shape=[out, out, out],
        compiler_params=pltpu.CompilerParams(
            dimension_semantics=("arbitrary",), vmem_limit_bytes=vmem),
        name="qkv_rope",
    )(x2, pos3, g.reshape(1, d), invf, w_qkv.astype(BF16))


def _attn_kernel(q_ref, k_ref, v_ref, x_ref, wo_ref, sg_ref, lq1_ref, lk1_ref,
                 lq2_ref, lk2_ref, o_ref, heads_ref, s0_ref, s1_ref, e0_ref, e1_ref,
                 *, lambda_init):
    tq = ATTN_UNIT_ROWS
    units = [(h, r) for h in range(N_HEADS) for r in range(q_ref.shape[0] // tq)]
    s_refs = (s0_ref, s1_ref)
    e_refs = (e0_ref, e1_ref)
    lam = (jnp.exp(jnp.sum(lq1_ref[...] * lk1_ref[...], axis=1, keepdims=True))
           - jnp.exp(jnp.sum(lq2_ref[...] * lk2_ref[...], axis=1, keepdims=True))
           + lambda_init)
    lane = lax.broadcasted_iota(jnp.int32, (1, LANES), 1)
    map1 = lane < HEAD_DIM
    sub_g = sg_ref[...] * (1.0 - lambda_init)

    def head_cols(h):
        return slice(h * V_DIM, (h + 1) * V_DIM)

    def scores(u):
        h, r = units[u]
        qh = q_ref[r * tq:(r + 1) * tq, head_cols(h)]
        zero = jnp.zeros_like(qh)
        q_both = jnp.concatenate([jnp.where(map1, qh, zero), jnp.where(map1, zero, qh)], axis=0)
        s_refs[u % 2][...] = lax.dot_general(
            q_both, k_ref[:, head_cols(h)], (((1,), (1,)), ((), ())),
            preferred_element_type=F32)

    def softmax(u):
        s = s_refs[u % 2][...]
        shifted = (s - jnp.max(s, axis=1, keepdims=True)).astype(BF16)
        e_refs[u % 2][...] = jnp.exp2(shifted)

    ones = jnp.ones((k_ref.shape[0], V_DIM), BF16)

    def values(u):
        h, r = units[u]
        v_ext = jnp.concatenate([v_ref[:, head_cols(h)], ones], axis=1)
        pv = jnp.dot(e_refs[u % 2][...], v_ext, preferred_element_type=F32)
        p = pv[:, :V_DIM] / pv[:, V_DIM:]
        o = p[:tq] - lam * p[tq:]
        heads_ref[r * tq:(r + 1) * tq, head_cols(h)] = _rms(o, sub_g).astype(BF16)

    scores(0)
    for t in range(len(units)):
        if t + 1 < len(units):
            scores(t + 1)
        softmax(t)
        if t >= 1:
            values(t - 1)
    values(len(units) - 1)

    o_ref[...] = x_ref[...] + jnp.dot(heads_ref[...], wo_ref[...], preferred_element_type=F32)


def _attention(x2, q, k, v, w_o, sub_g, lq1, lk1, lq2, lk2, *, seq, lambda_init):
    t, d = x2.shape
    rows, tq = ATTN_Q_ROWS, ATTN_UNIT_ROWS
    nq = seq // rows
    q_spec = pl.BlockSpec((rows, d), lambda b, i: (b * nq + i, 0))
    kv_spec = pl.BlockSpec((seq, d), lambda b, i: (b, 0))
    vec = lambda n: pl.BlockSpec((1, n), lambda b, i: (0, 0))
    scratch = [((rows, d), BF16), ((2 * tq, seq), F32), ((2 * tq, seq), F32),
               ((2 * tq, seq), BF16), ((2 * tq, seq), BF16)]
    vmem = _vmem_limit(
        pipelined=[_nbytes((rows, d), BF16), 2 * _nbytes((seq, d), BF16),
                   2 * _nbytes((rows, d), F32), _nbytes((d, d), BF16)],
        single=[_nbytes(*s) for s in scratch] + [_nbytes((seq, 2 * V_DIM), BF16)])
    return pl.pallas_call(
        functools.partial(_attn_kernel, lambda_init=lambda_init),
        grid=(t // seq, nq),
        in_specs=[q_spec, kv_spec, kv_spec, q_spec,
                  pl.BlockSpec((d, d), lambda b, i: (0, 0)),
                  vec(V_DIM), vec(HEAD_DIM), vec(HEAD_DIM), vec(HEAD_DIM), vec(HEAD_DIM)],
        out_specs=q_spec,
        out_shape=jax.ShapeDtypeStruct((t, d), F32),
        scratch_shapes=[pltpu.VMEM(*s) for s in scratch],
        compiler_params=pltpu.CompilerParams(
            dimension_semantics=("arbitrary", "arbitrary"), vmem_limit_bytes=vmem),
        name="diff_attn",
    )(q, k, v, x2, w_o.astype(BF16), sub_g.reshape(1, V_DIM),
      lq1.reshape(1, HEAD_DIM), lk1.reshape(1, HEAD_DIM),
      lq2.reshape(1, HEAD_DIM), lk2.reshape(1, HEAD_DIM))


def _gated_block_kernel(*refs, n_up, n_col, mixer, final_norm, blocks_per_seq, down_group):
    x_ref, xprev_ref, xnext_ref, g_ref, wup_ref, cw_ref = refs[:6]
    pos = 6
    if not mixer:
        cb_ref = refs[pos]
        pos += 1
    wd_ref = refs[pos]
    pos += 1
    if final_norm:
        fg_ref = refs[pos]
        pos += 1
    o_ref, h_ref = refs[pos:pos + 2]
    u_refs = refs[pos + 2:pos + 4]
    z_refs = refs[pos + 4:pos + 6]

    rows = x_ref.shape[0]
    cols = wd_ref.shape[1]

    n_chunks = rows // CONV_CHUNK
    g = g_ref[...]
    r = pl.program_id(0) % blocks_per_seq
    h_prev = jnp.where(r == 0, 0.0, _rms(xprev_ref[...], g))
    h_next = jnp.where(r == blocks_per_seq - 1, 0.0, _rms(xnext_ref[...], g))
    h_ref[0:HALO, :] = h_prev.astype(BF16)
    h_ref[HALO + rows:, :] = h_next.astype(BF16)
    for c in range(rows // NORM_ROWS):
        x = x_ref[c * NORM_ROWS:(c + 1) * NORM_ROWS, :]
        h_ref[HALO + c * NORM_ROWS:HALO + (c + 1) * NORM_ROWS, :] = _rms(x, g).astype(BF16)
        o_ref[c * NORM_ROWS:(c + 1) * NORM_ROWS, :] = x

    def h_span(c):
        lo = 0 if c == 0 else HALO + c * CONV_CHUNK
        hi = rows + 2 * HALO if c == n_chunks - 1 else HALO + (c + 1) * CONV_CHUNK
        return lo, hi

    def up(j, by_chunk=False):
        spans = [h_span(c) for c in range(n_chunks)] if by_chunk else [(0, rows + 2 * HALO)]
        for lo, hi in spans:
            h = h_ref[lo:hi, :]
            for n in range(n_up):
                u_refs[j % 2][lo:hi, n * cols:(n + 1) * cols] = jnp.dot(
                    h, wup_ref[n * n_col + j], preferred_element_type=F32)

    def act(j):
        u = u_refs[j % 2]

        def window(part, r0):
            return u[r0 - SUBLANES:r0 + ACT_ROWS + SUBLANES, part * cols:(part + 1) * cols]

        def conv(win, part):
            n = win.shape[0]
            cw = cw_ref[part * n_col + j]
            inner = slice(SUBLANES, SUBLANES + ACT_ROWS)
            return (cw[0:1] * pltpu.roll(win, 1, 0)[inner] + cw[1:2] * win[inner]
                    + cw[2:3] * pltpu.roll(win, n - 1, 0)[inner])

        for c in range(rows // ACT_ROWS):
            r0 = HALO + c * ACT_ROWS
            if mixer:
                gate = u[r0:r0 + ACT_ROWS, 0:cols]
                z = gate * conv(window(1, r0) * window(2, r0), 0)
            else:
                yg = conv(window(0, r0), 0) + cb_ref[j]
                yv = conv(window(1, r0), 1) + cb_ref[n_col + j]
                z = yg * jax.nn.sigmoid(yg) * yv
            group, slot = divmod(j, down_group)
            z_refs[group % 2][c * ACT_ROWS:(c + 1) * ACT_ROWS,
                              slot * cols:(slot + 1) * cols] = z.astype(BF16)

    def down(tiles, row_chunk=None, last=False):
        group = tiles[0] // down_group
        k = len(tiles) * cols
        wd = wd_ref[tiles[0]:tiles[0] + len(tiles)].reshape(k, wd_ref.shape[2])
        step = rows if row_chunk is None else row_chunk
        for r0 in range(0, rows, step):
            rs = slice(r0, r0 + step)
            o_ref[rs, :] += jnp.dot(z_refs[group % 2][rs, 0:k], wd, preferred_element_type=F32)
            if last and final_norm:
                for p in range(r0, r0 + step, NORM_ROWS):
                    o_ref[p:p + NORM_ROWS, :] = _rms(o_ref[p:p + NORM_ROWS, :], fg_ref[...])

    up(0, by_chunk=True)
    done = 0
    for s in range(n_col):
        if s + 1 < n_col:
            up(s + 1)
        act(s)
        if s >= down_group and s % down_group == 0:
            down(list(range(s - down_group, s)))
            done = s
    down(list(range(done, n_col)), row_chunk=CONV_CHUNK, last=True)


def _gated_block(x2, norm_g, w_up, conv_w, conv_b, w_down, final_g, *, seq, mixer):
    t, d = x2.shape
    hidden = w_down.shape[0]
    n_up = w_up.shape[1] // hidden
    n_conv = conv_w.shape[1] // hidden
    rows, cols = BLOCK_ROWS, BLOCK_COLS
    n_col = hidden // cols
    blocks_per_seq = seq // rows
    halo_per_block = rows // HALO
    last_halo = t // HALO - 1
    final_norm = final_g is not None

    row_spec = pl.BlockSpec((rows, d), lambda i: (i, 0))
    prev_spec = pl.BlockSpec((HALO, d), lambda i: (jnp.maximum(i * halo_per_block - 1, 0), 0))
    next_spec = pl.BlockSpec((HALO, d), lambda i: (jnp.minimum((i + 1) * halo_per_block, last_halo), 0))

    def resident(a):
        zeros = (0,) * a.ndim
        return pl.BlockSpec(a.shape, lambda i: zeros, pipeline_mode=pl.Buffered(1))

    w_up_t = w_up.astype(BF16).reshape(d, n_up * n_col, cols).transpose(1, 0, 2)
    conv_w_t = conv_w.reshape(3, n_conv * n_col, cols).transpose(1, 0, 2)
    w_down_t = w_down.astype(BF16).reshape(n_col, cols, d)
    weights = [norm_g.reshape(1, d), w_up_t, conv_w_t]
    if not mixer:
        weights.append(conv_b.reshape(n_conv * n_col, 1, cols))
    weights.append(w_down_t)
    if final_norm:
        weights.append(final_g.reshape(1, d))

    down_group = 1 if mixer else 5
    scratch = [((rows + 2 * HALO, d), BF16),
               ((rows + 2 * HALO, n_up * cols), F32), ((rows + 2 * HALO, n_up * cols), F32),
               ((rows, down_group * cols), BF16), ((rows, down_group * cols), BF16)]
    vmem = _vmem_limit(
        pipelined=[2 * _nbytes((rows, d), F32), 2 * _nbytes((HALO, d), F32)],
        single=[_nbytes(w.shape, w.dtype) for w in weights] + [_nbytes(*s) for s in scratch]
        + [_nbytes((rows, d), F32)])
    return pl.pallas_call(
        functools.partial(_gated_block_kernel, n_up=n_up, n_col=n_col, mixer=mixer,
                          final_norm=final_norm, blocks_per_seq=blocks_per_seq,
                          down_group=down_group),
        grid=(t // rows,),
        in_specs=[row_spec, prev_spec, next_spec] + [resident(w) for w in weights],
        out_specs=row_spec,
        out_shape=jax.ShapeDtypeStruct((t, d), F32),
        scratch_shapes=[pltpu.VMEM(*s) for s in scratch],
        compiler_params=pltpu.CompilerParams(
            dimension_semantics=("arbitrary",), vmem_limit_bytes=vmem),
        name="mixer_block" if mixer else "conv_ffn",
    )(x2, x2, x2, *weights)


def kernel(x, positions, attn_norm_g, attn_w_qkv, attn_lambda_q1, attn_lambda_k1, attn_lambda_q2, attn_lambda_k2, attn_subln_g, attn_w_o, conv_norm_g, conv_w_in, conv_w, conv_w_out, ffn_norm_g, ffn_w_up, ffn_conv_w, ffn_conv_b, ffn_w_down, final_norm_g):
    batch, seq, d = x.shape
    depth = ffn_norm_g.shape[0]
    assert d == N_HEADS * V_DIM
    x2 = x.reshape(batch * seq, d)
    for i in range(depth):
        if i % N_MIXERS == 0:
            a = i // N_MIXERS
            lambda_init = 0.8 - 0.6 * math.exp(-0.3 * i)
            q, k, v = _qkv_proj(x2, positions, attn_norm_g[a], attn_w_qkv[a])
            x2 = _attention(x2, q, k, v, attn_w_o[a], attn_subln_g[a],
                            attn_lambda_q1[a], attn_lambda_k1[a],
                            attn_lambda_q2[a], attn_lambda_k2[a],
                            seq=seq, lambda_init=lambda_init)
        else:
            c = i // N_MIXERS
            x2 = _gated_block(x2, conv_norm_g[c], conv_w_in[c], conv_w[c], None,
                              conv_w_out[c], None, seq=seq, mixer=True)
        x2 = _gated_block(x2, ffn_norm_g[i], ffn_w_up[i], ffn_conv_w[i], ffn_conv_b[i],
                          ffn_w_down[i], final_norm_g if i == depth - 1 else None,
                          seq=seq, mixer=False)
    return x2.reshape(batch, seq, d)
```

```python
import functools
import math

import jax
import jax.numpy as jnp
from jax import lax
from jax.experimental import pallas as pl
from jax.experimental.pallas import tpu as pltpu

N_HEADS = 8
HEAD_DIM = 64
V_DIM = 2 * HEAD_DIM
ROPE_THETA = 10000.0
RMS_EPS = 1e-5
N_MIXERS = 2

F32 = jnp.float32
BF16 = jnp.bfloat16

LANES = 128
SUBLANES = 8
BF16_SUBLANES = 16
V7X_VMEM_BYTES = 64 * 1024 * 1024

QKV_ROWS = 512
ATTN_Q_ROWS = 512
ATTN_UNIT_ROWS = 256
BLOCK_ROWS = 1024
BLOCK_COLS = 256
CONV_CHUNK = 256
ACT_ROWS = 512
NORM_ROWS = 32
HALO = BF16_SUBLANES


def _nbytes(shape, dtype):
    return math.prod(shape) * jnp.dtype(dtype).itemsize


def _vmem_limit(pipelined, single):
    need = 2 * sum(pipelined) + sum(single)
    assert need <= V7X_VMEM_BYTES, need
    return need


def _rms(x, g):
    return x * lax.rsqrt(jnp.mean(x * x, axis=-1, keepdims=True) + RMS_EPS) * g


def _qkv_kernel(x_ref, pos_ref, g_ref, invf_ref, w_ref, q_ref, k_ref, v_ref):
    d = x_ref.shape[1]
    h = _rms(x_ref[...], g_ref[...]).astype(BF16)
    acc = jnp.dot(h, w_ref[...], preferred_element_type=F32)

    ang_t = invf_ref[...] * pos_ref[...].astype(F32)
    quarters = LANES // (HEAD_DIM // 2)
    cos = jnp.concatenate([jnp.cos(ang_t)] * quarters, axis=0).T
    sin = jnp.concatenate([jnp.sin(ang_t)] * quarters, axis=0).T
    lane = lax.broadcasted_iota(jnp.int32, (1, LANES), 1)
    first_half = (lane & (HEAD_DIM // 2)) == 0
    sin_signed = jnp.where(first_half, -sin, sin)
    q_scale = HEAD_DIM ** -0.5 * math.log2(math.e)
    cos_q, sin_q = cos * q_scale, sin_signed * q_scale

    n_groups = d // LANES
    for g in range(2 * n_groups):
        xg = acc[:, g * LANES:(g + 1) * LANES]
        swapped = jnp.where(first_half,
                            pltpu.roll(xg, LANES - HEAD_DIM // 2, 1),
                            pltpu.roll(xg, HEAD_DIM // 2, 1))
        if g < n_groups:
            q_ref[:, g * LANES:(g + 1) * LANES] = (xg * cos_q + swapped * sin_q).astype(BF16)
        else:
            c0 = (g - n_groups) * LANES
            k_ref[:, c0:c0 + LANES] = (xg * cos + swapped * sin_signed).astype(BF16)
    v_ref[...] = acc[:, 2 * d:].astype(BF16)


def _qkv_proj(x2, positions, g, w_qkv):
    t, d = x2.shape
    rows = QKV_ROWS
    inv_freq = ROPE_THETA ** (-jnp.arange(0, HEAD_DIM, 2, dtype=F32) / HEAD_DIM)
    invf = inv_freq.reshape(HEAD_DIM // 2, 1)
    pos3 = positions.reshape(t // rows, 1, rows)
    out = jax.ShapeDtypeStruct((t, d), BF16)
    row_spec = pl.BlockSpec((rows, d), lambda i: (i, 0))
    vmem = _vmem_limit(
        pipelined=[_nbytes((rows, d), F32), 3 * _nbytes((rows, d), BF16),
                   _nbytes((d, 3 * d), BF16)],
        single=[_nbytes((rows, 3 * d), F32), _nbytes((rows, d), BF16)])
    return pl.pallas_call(
        _qkv_kernel,
        grid=(t // rows,),
        in_specs=[
            row_spec,
            pl.BlockSpec((None, 1, rows), lambda i: (i, 0, 0)),
            pl.BlockSpec((1, d), lambda i: (0, 0)),
            pl.BlockSpec((HEAD_DIM // 2, 1), lambda i: (0, 0)),
            pl.BlockSpec((d, 3 * d), lambda i: (0, 0)),
        ],
        out_specs=[row_spec, row_spec, row_spec],
        out_shape=[out, out, out],
        compiler_params=pltpu.CompilerParams(
            dimension_semantics=("arbitrary",), vmem_limit_bytes=vmem),
        name="qkv_rope",
    )(x2, pos3, g.reshape(1, d), invf, w_qkv.astype(BF16))


def _attn_kernel(q_ref, k_ref, v_ref, x_ref, wo_ref, sg_ref, lq1_ref, lk1_ref,
                 lq2_ref, lk2_ref, o_ref, heads_ref, s0_ref, s1_ref, e0_ref, e1_ref,
                 *, lambda_init):
    tq = ATTN_UNIT_ROWS
    units = [(h, r) for h in range(N_HEADS) for r in range(q_ref.shape[0] // tq)]
    s_refs = (s0_ref, s1_ref)
    e_refs = (e0_ref, e1_ref)
    lam = (jnp.exp(jnp.sum(lq1_ref[...] * lk1_ref[...], axis=1, keepdims=True))
           - jnp.exp(jnp.sum(lq2_ref[...] * lk2_ref[...], axis=1, keepdims=True))
           + lambda_init)
    lane = lax.broadcasted_iota(jnp.int32, (1, LANES), 1)
    map1 = lane < HEAD_DIM
    sub_g = sg_ref[...] * (1.0 - lambda_init)

    def head_cols(h):
        return slice(h * V_DIM, (h + 1) * V_DIM)

    def scores(u):
        h, r = units[u]
        qh = q_ref[r * tq:(r + 1) * tq, head_cols(h)]
        zero = jnp.zeros_like(qh)
        q_both = jnp.concatenate([jnp.where(map1, qh, zero), jnp.where(map1, zero, qh)], axis=0)
        s_refs[u % 2][...] = lax.dot_general(
            q_both, k_ref[:, head_cols(h)], (((1,), (1,)), ((), ())),
            preferred_element_type=F32)

    def softmax(u):
        s = s_refs[u % 2][...]
        shifted = (s - jnp.max(s, axis=1, keepdims=True)).astype(BF16)
        e_refs[u % 2][...] = jnp.exp2(shifted)

    ones = jnp.ones((k_ref.shape[0], V_DIM), BF16)

    def values(u):
        h, r = units[u]
        v_ext = jnp.concatenate([v_ref[:, head_cols(h)], ones], axis=1)
        pv = jnp.dot(e_refs[u % 2][...], v_ext, preferred_element_type=F32)
        p = pv[:, :V_DIM] / pv[:, V_DIM:]
        o = p[:tq] - lam * p[tq:]
        heads_ref[r * tq:(r + 1) * tq, head_cols(h)] = _rms(o, sub_g).astype(BF16)

    scores(0)
    for t in range(len(units)):
        if t + 1 < len(units):
            scores(t + 1)
        softmax(t)
        if t >= 1:
            values(t - 1)
    values(len(units) - 1)

    o_ref[...] = x_ref[...] + jnp.dot(heads_ref[...], wo_ref[...], preferred_element_type=F32)


def _attention(x2, q, k, v, w_o, sub_g, lq1, lk1, lq2, lk2, *, seq, lambda_init):
    t, d = x2.shape
    rows, tq = ATTN_Q_ROWS, ATTN_UNIT_ROWS
    nq = seq // rows
    q_spec = pl.BlockSpec((rows, d), lambda b, i: (b * nq + i, 0))
    kv_spec = pl.BlockSpec((seq, d), lambda b, i: (b, 0))
    vec = lambda n: pl.BlockSpec((1, n), lambda b, i: (0, 0))
    scratch = [((rows, d), BF16), ((2 * tq, seq), F32), ((2 * tq, seq), F32),
               ((2 * tq, seq), BF16), ((2 * tq, seq), BF16)]
    vmem = _vmem_limit(
        pipelined=[_nbytes((rows, d), BF16), 2 * _nbytes((seq, d), BF16),
                   2 * _nbytes((rows, d), F32), _nbytes((d, d), BF16)],
        single=[_nbytes(*s) for s in scratch] + [_nbytes((seq, 2 * V_DIM), BF16)])
    return pl.pallas_call(
        functools.partial(_attn_kernel, lambda_init=lambda_init),
        grid=(t // seq, nq),
        in_specs=[q_spec, kv_spec, kv_spec, q_spec,
                  pl.BlockSpec((d, d), lambda b, i: (0, 0)),
                  vec(V_DIM), vec(HEAD_DIM), vec(HEAD_DIM), vec(HEAD_DIM), vec(HEAD_DIM)],
        out_specs=q_spec,
        out_shape=jax.ShapeDtypeStruct((t, d), F32),
        scratch_shapes=[pltpu.VMEM(*s) for s in scratch],
        compiler_params=pltpu.CompilerParams(
            dimension_semantics=("arbitrary", "arbitrary"), vmem_limit_bytes=vmem),
        name="diff_attn",
    )(q, k, v, x2, w_o.astype(BF16), sub_g.reshape(1, V_DIM),
      lq1.reshape(1, HEAD_DIM), lk1.reshape(1, HEAD_DIM),
      lq2.reshape(1, HEAD_DIM), lk2.reshape(1, HEAD_DIM))


def _gated_block_kernel(*refs, n_up, n_col, mixer, final_norm, blocks_per_seq, down_group):
    x_ref, xprev_ref, xnext_ref, g_ref, wup_ref, cw_ref = refs[:6]
    pos = 6
    if not mixer:
        cb_ref = refs[pos]
        pos += 1
    wd_ref = refs[pos]
    pos += 1
    if final_norm:
        fg_ref = refs[pos]
        pos += 1
    o_ref, h_ref = refs[pos:pos + 2]
    u_refs = refs[pos + 2:pos + 4]
    z_refs = refs[pos + 4:pos + 6]

    rows = x_ref.shape[0]
    cols = wd_ref.shape[0] // n_col

    def tile_cols(part, j):
        c0 = (part * n_col + j) * cols
        return slice(c0, c0 + cols)

    n_chunks = rows // CONV_CHUNK
    g = g_ref[...]
    r = pl.program_id(0) % blocks_per_seq
    h_prev = jnp.where(r == 0, 0.0, _rms(xprev_ref[...], g))
    h_next = jnp.where(r == blocks_per_seq - 1, 0.0, _rms(xnext_ref[...], g))
    h_ref[0:HALO, :] = h_prev.astype(BF16)
    h_ref[HALO + rows:, :] = h_next.astype(BF16)
    for c in range(rows // NORM_ROWS):
        x = x_ref[c * NORM_ROWS:(c + 1) * NORM_ROWS, :]
        h_ref[HALO + c * NORM_ROWS:HALO + (c + 1) * NORM_ROWS, :] = _rms(x, g).astype(BF16)
        o_ref[c * NORM_ROWS:(c + 1) * NORM_ROWS, :] = x

    def h_span(c):
        lo = 0 if c == 0 else HALO + c * CONV_CHUNK
        hi = rows + 2 * HALO if c == n_chunks - 1 else HALO + (c + 1) * CONV_CHUNK
        return lo, hi

    def up(j, by_chunk=False):
        spans = [h_span(c) for c in range(n_chunks)] if by_chunk else [(0, rows + 2 * HALO)]
        for lo, hi in spans:
            h = h_ref[lo:hi, :]
            for n in range(n_up):
                u_refs[j % 2][lo:hi, n * cols:(n + 1) * cols] = jnp.dot(
                    h, wup_ref[:, tile_cols(n, j)], preferred_element_type=F32)

    def act(j):
        u = u_refs[j % 2]

        def window(part, r0):
            return u[r0 - SUBLANES:r0 + ACT_ROWS + SUBLANES, part * cols:(part + 1) * cols]

        def conv(win, part):
            n = win.shape[0]
            cw = cw_ref[:, tile_cols(part, j)]
            inner = slice(SUBLANES, SUBLANES + ACT_ROWS)
            return (cw[0:1] * pltpu.roll(win, 1, 0)[inner] + cw[1:2] * win[inner]
                    + cw[2:3] * pltpu.roll(win, n - 1, 0)[inner])

        for c in range(rows // ACT_ROWS):
            r0 = HALO + c * ACT_ROWS
            if mixer:
                gate = u[r0:r0 + ACT_ROWS, 0:cols]
                z = gate * conv(window(1, r0) * window(2, r0), 0)
            else:
                yg = conv(window(0, r0), 0) + cb_ref[:, tile_cols(0, j)]
                yv = conv(window(1, r0), 1) + cb_ref[:, tile_cols(1, j)]
                z = yg * jax.nn.sigmoid(yg) * yv
            group, slot = divmod(j, down_group)
            z_refs[group % 2][c * ACT_ROWS:(c + 1) * ACT_ROWS,
                              slot * cols:(slot + 1) * cols] = z.astype(BF16)

    def down(tiles, row_chunk=None, last=False):
        group = tiles[0] // down_group
        k = len(tiles) * cols
        wd = wd_ref[tiles[0] * cols:tiles[0] * cols + k, :]
        step = rows if row_chunk is None else row_chunk
        for r0 in range(0, rows, step):
            rs = slice(r0, r0 + step)
            o_ref[rs, :] += jnp.dot(z_refs[group % 2][rs, 0:k], wd, preferred_element_type=F32)
            if last and final_norm:
                for p in range(r0, r0 + step, NORM_ROWS):
                    o_ref[p:p + NORM_ROWS, :] = _rms(o_ref[p:p + NORM_ROWS, :], fg_ref[...])

    up(0, by_chunk=True)
    done = 0
    for s in range(n_col):
        if s + 1 < n_col:
            up(s + 1)
        act(s)
        if s >= down_group and s % down_group == 0:
            down(list(range(s - down_group, s)))
            done = s
    down(list(range(done, n_col)), row_chunk=CONV_CHUNK, last=True)


def _gated_block(x2, norm_g, w_up, conv_w, conv_b, w_down, final_g, *, seq, mixer):
    t, d = x2.shape
    hidden = w_down.shape[0]
    n_up = w_up.shape[1] // hidden
    n_conv = conv_w.shape[1] // hidden
    rows, cols = BLOCK_ROWS, BLOCK_COLS
    n_col = hidden // cols
    blocks_per_seq = seq // rows
    halo_per_block = rows // HALO
    last_halo = t // HALO - 1
    final_norm = final_g is not None

    row_spec = pl.BlockSpec((rows, d), lambda i: (i, 0))
    prev_spec = pl.BlockSpec((HALO, d), lambda i: (jnp.maximum(i * halo_per_block - 1, 0), 0))
    next_spec = pl.BlockSpec((HALO, d), lambda i: (jnp.minimum((i + 1) * halo_per_block, last_halo), 0))

    def resident(a):
        zeros = (0,) * a.ndim
        return pl.BlockSpec(a.shape, lambda i: zeros, pipeline_mode=pl.Buffered(1))

    weights = [norm_g.reshape(1, d), w_up.astype(BF16), conv_w]
    if not mixer:
        weights.append(conv_b.reshape(1, n_conv * hidden))
    weights.append(w_down.astype(BF16))
    if final_norm:
        weights.append(final_g.reshape(1, d))

    down_group = 1 if mixer else 5
    scratch = [((rows + 2 * HALO, d), BF16),
               ((rows + 2 * HALO, n_up * cols), F32), ((rows + 2 * HALO, n_up * cols), F32),
               ((rows, down_group * cols), BF16), ((rows, down_group * cols), BF16)]
    vmem = _vmem_limit(
        pipelined=[2 * _nbytes((rows, d), F32), 2 * _nbytes((HALO, d), F32)],
        single=[_nbytes(w.shape, w.dtype) for w in weights] + [_nbytes(*s) for s in scratch]
        + [_nbytes((rows, d), F32)])
    return pl.pallas_call(
        functools.partial(_gated_block_kernel, n_up=n_up, n_col=n_col, mixer=mixer,
                          final_norm=final_norm, blocks_per_seq=blocks_per_seq,
                          down_group=down_group),
        grid=(t // rows,),
        in_specs=[row_spec, prev_spec, next_spec] + [resident(w) for w in weights],
        out_specs=row_spec,
        out_shape=jax.ShapeDtypeStruct((t, d), F32),
        scratch_shapes=[pltpu.VMEM(*s) for s in scratch],
        compiler_params=pltpu.CompilerParams(
            dimension_semantics=("arbitrary",), vmem_limit_bytes=vmem),
        name="mixer_block" if mixer else "conv_ffn",
    )(x2, x2, x2, *weights)


def kernel(x, positions, attn_norm_g, attn_w_qkv, attn_lambda_q1, attn_lambda_k1, attn_lambda_q2, attn_lambda_k2, attn_subln_g, attn_w_o, conv_norm_g, conv_w_in, conv_w, conv_w_out, ffn_norm_g, ffn_w_up, ffn_conv_w, ffn_conv_b, ffn_w_down, final_norm_g):
    batch, seq, d = x.shape
    depth = ffn_norm_g.shape[0]
    assert d == N_HEADS * V_DIM
    x2 = x.reshape(batch * seq, d)
    for i in range(depth):
        if i % N_MIXERS == 0:
            a = i // N_MIXERS
            lambda_init = 0.8 - 0.6 * math.exp(-0.3 * i)
            q, k, v = _qkv_proj(x2, positions, attn_norm_g[a], attn_w_qkv[a])
            x2 = _attention(x2, q, k, v, attn_w_o[a], attn_subln_g[a],
                            attn_lambda_q1[a], attn_lambda_k1[a],
                            attn_lambda_q2[a], attn_lambda_k2[a],
                            seq=seq, lambda_init=lambda_init)
        else:
            c = i // N_MIXERS
            x2 = _gated_block(x2, conv_norm_g[c], conv_w_in[c], conv_w[c], None,
                              conv_w_out[c], None, seq=seq, mixer=True)
        x2 = _gated_block(x2, ffn_norm_g[i], ffn_w_up[i], ffn_conv_w[i], ffn_conv_b[i],
                          ffn_w_down[i], final_norm_g if i == depth - 1 else None,
                          seq=seq, mixer=False)
    return x2.reshape(batch, seq, d)
```

```python
import functools
import math

import jax
import jax.numpy as jnp
from jax import lax
from jax.experimental import pallas as pl
from jax.experimental.pallas import tpu as pltpu

N_HEADS = 8
HEAD_DIM = 64
V_DIM = 2 * HEAD_DIM
ROPE_THETA = 10000.0
RMS_EPS = 1e-5
N_MIXERS = 2

F32 = jnp.float32
BF16 = jnp.bfloat16

LANES = 128
SUBLANES = 8
BF16_SUBLANES = 16
V7X_VMEM_BYTES = 64 * 1024 * 1024

QKV_ROWS = 512
ATTN_Q_ROWS = 512
ATTN_UNIT_ROWS = 256
BLOCK_ROWS = 1024
BLOCK_COLS = 256
CONV_CHUNK = 256
ACT_ROWS = 512
NORM_ROWS = 32
HALO = BF16_SUBLANES


def _nbytes(shape, dtype):
    return math.prod(shape) * jnp.dtype(dtype).itemsize


def _vmem_limit(pipelined, single):
    need = 2 * sum(pipelined) + sum(single)
    assert need <= V7X_VMEM_BYTES, need
    return need


def _rms(x, g):
    return x * lax.rsqrt(jnp.mean(x * x, axis=-1, keepdims=True) + RMS_EPS) * g


def _qkv_kernel(x_ref, pos_ref, g_ref, invf_ref, w_ref, q_ref, k_ref, v_ref):
    d = x_ref.shape[1]
    h = _rms(x_ref[...], g_ref[...]).astype(BF16)
    acc = jnp.dot(h, w_ref[...], preferred_element_type=F32)

    ang_t = invf_ref[...] * pos_ref[...].astype(F32)
    quarters = LANES // (HEAD_DIM // 2)
    cos = jnp.concatenate([jnp.cos(ang_t)] * quarters, axis=0).T
    sin = jnp.concatenate([jnp.sin(ang_t)] * quarters, axis=0).T
    lane = lax.broadcasted_iota(jnp.int32, (1, LANES), 1)
    first_half = (lane & (HEAD_DIM // 2)) == 0
    sin_signed = jnp.where(first_half, -sin, sin)
    q_scale = HEAD_DIM ** -0.5 * math.log2(math.e)
    cos_q, sin_q = cos * q_scale, sin_signed * q_scale

    n_groups = d // LANES
    for g in range(2 * n_groups):
        xg = acc[:, g * LANES:(g + 1) * LANES]
        swapped = jnp.where(first_half,
                            pltpu.roll(xg, LANES - HEAD_DIM // 2, 1),
                            pltpu.roll(xg, HEAD_DIM // 2, 1))
        if g < n_groups:
            q_ref[:, g * LANES:(g + 1) * LANES] = (xg * cos_q + swapped * sin_q).astype(BF16)
        else:
            c0 = (g - n_groups) * LANES
            k_ref[:, c0:c0 + LANES] = (xg * cos + swapped * sin_signed).astype(BF16)
    v_ref[...] = acc[:, 2 * d:].astype(BF16)


def _qkv_proj(x2, positions, g, w_qkv):
    t, d = x2.shape
    rows = QKV_ROWS
    inv_freq = ROPE_THETA ** (-jnp.arange(0, HEAD_DIM, 2, dtype=F32) / HEAD_DIM)
    invf = inv_freq.reshape(HEAD_DIM // 2, 1)
    pos3 = positions.reshape(t // rows, 1, rows)
    out = jax.ShapeDtypeStruct((t, d), BF16)
    row_spec = pl.BlockSpec((rows, d), lambda i: (i, 0))
    vmem = _vmem_limit(
        pipelined=[_nbytes((rows, d), F32), 3 * _nbytes((rows, d), BF16),
                   _nbytes((d, 3 * d), BF16)],
        single=[_nbytes((rows, 3 * d), F32), _nbytes((rows, d), BF16)])
    return pl.pallas_call(
        _qkv_kernel,
        grid=(t // rows,),
        in_specs=[
            row_spec,
            pl.BlockSpec((None, 1, rows), lambda i: (i, 0, 0)),
            pl.BlockSpec((1, d), lambda i: (0, 0)),
            pl.BlockSpec((HEAD_DIM // 2, 1), lambda i: (0, 0)),
            pl.BlockSpec((d, 3 * d), lambda i: (0, 0)),
        ],
        out_specs=[row_spec, row_spec, row_spec],
        out_shape=[out, out, out],
        compiler_params=pltpu.CompilerParams(
            dimension_semantics=("arbitrary",), vmem_limit_bytes=vmem),
        name="qkv_rope",
    )(x2, pos3, g.reshape(1, d), invf, w_qkv.astype(BF16))


def _attn_kernel(q_ref, k_ref, v_ref, x_ref, wo_ref, sg_ref, lq1_ref, lk1_ref,
                 lq2_ref, lk2_ref, o_ref, heads_ref, s0_ref, s1_ref, e0_ref, e1_ref,
                 *, lambda_init):
    tq = ATTN_UNIT_ROWS
    units = [(h, r) for r in range(q_ref.shape[0] // tq) for h in range(N_HEADS)]
    s_refs = (s0_ref, s1_ref)
    e_refs = (e0_ref, e1_ref)
    lam = (jnp.exp(jnp.sum(lq1_ref[...] * lk1_ref[...], axis=1, keepdims=True))
           - jnp.exp(jnp.sum(lq2_ref[...] * lk2_ref[...], axis=1, keepdims=True))
           + lambda_init)
    lane = lax.broadcasted_iota(jnp.int32, (1, LANES), 1)
    map1 = lane < HEAD_DIM
    sub_g = sg_ref[...] * (1.0 - lambda_init)

    def head_cols(h):
        return slice(h * V_DIM, (h + 1) * V_DIM)

    def scores(u):
        h, r = units[u]
        qh = q_ref[r * tq:(r + 1) * tq, head_cols(h)]
        zero = jnp.zeros_like(qh)
        q_both = jnp.concatenate([jnp.where(map1, qh, zero), jnp.where(map1, zero, qh)], axis=0)
        s_refs[u % 2][...] = lax.dot_general(
            q_both, k_ref[:, head_cols(h)], (((1,), (1,)), ((), ())),
            preferred_element_type=F32)

    def softmax(u):
        s = s_refs[u % 2][...]
        shifted = (s - jnp.max(s, axis=1, keepdims=True)).astype(BF16)
        e_refs[u % 2][...] = jnp.exp2(shifted)

    ones = jnp.ones((k_ref.shape[0], V_DIM), BF16)

    def values(u):
        h, r = units[u]
        v_ext = jnp.concatenate([v_ref[:, head_cols(h)], ones], axis=1)
        pv = jnp.dot(e_refs[u % 2][...], v_ext, preferred_element_type=F32)
        p = pv[:, :V_DIM] / pv[:, V_DIM:]
        o = p[:tq] - lam * p[tq:]
        heads_ref[r * tq:(r + 1) * tq, head_cols(h)] = _rms(o, sub_g).astype(BF16)

    def finish(u):
        values(u)
        h, r = units[u]
        if h == N_HEADS - 1:
            rs = slice(r * tq, (r + 1) * tq)
            o_ref[rs, :] = x_ref[rs, :] + jnp.dot(heads_ref[rs, :], wo_ref[...],
                                                  preferred_element_type=F32)

    scores(0)
    for t in range(len(units)):
        if t + 1 < len(units):
            scores(t + 1)
        softmax(t)
        if t >= 1:
            finish(t - 1)
    finish(len(units) - 1)


def _attention(x2, q, k, v, w_o, sub_g, lq1, lk1, lq2, lk2, *, seq, lambda_init):
    t, d = x2.shape
    rows, tq = ATTN_Q_ROWS, ATTN_UNIT_ROWS
    nq = seq // rows
    q_spec = pl.BlockSpec((rows, d), lambda b, i: (b * nq + i, 0))
    kv_spec = pl.BlockSpec((seq, d), lambda b, i: (b, 0))
    vec = lambda n: pl.BlockSpec((1, n), lambda b, i: (0, 0))
    scratch = [((rows, d), BF16), ((2 * tq, seq), F32), ((2 * tq, seq), F32),
               ((2 * tq, seq), BF16), ((2 * tq, seq), BF16)]
    vmem = _vmem_limit(
        pipelined=[_nbytes((rows, d), BF16), 2 * _nbytes((seq, d), BF16),
                   2 * _nbytes((rows, d), F32), _nbytes((d, d), BF16)],
        single=[_nbytes(*s) for s in scratch] + [_nbytes((seq, 2 * V_DIM), BF16)])
    return pl.pallas_call(
        functools.partial(_attn_kernel, lambda_init=lambda_init),
        grid=(t // seq, nq),
        in_specs=[q_spec, kv_spec, kv_spec, q_spec,
                  pl.BlockSpec((d, d), lambda b, i: (0, 0)),
                  vec(V_DIM), vec(HEAD_DIM), vec(HEAD_DIM), vec(HEAD_DIM), vec(HEAD_DIM)],
        out_specs=q_spec,
        out_shape=jax.ShapeDtypeStruct((t, d), F32),
        scratch_shapes=[pltpu.VMEM(*s) for s in scratch],
        compiler_params=pltpu.CompilerParams(
            dimension_semantics=("arbitrary", "arbitrary"), vmem_limit_bytes=vmem),
        name="diff_attn",
    )(q, k, v, x2, w_o.astype(BF16), sub_g.reshape(1, V_DIM),
      lq1.reshape(1, HEAD_DIM), lk1.reshape(1, HEAD_DIM),
      lq2.reshape(1, HEAD_DIM), lk2.reshape(1, HEAD_DIM))


def _gated_block_kernel(*refs, n_up, n_col, mixer, final_norm, blocks_per_seq, down_group):
    x_ref, xprev_ref, xnext_ref, g_ref, wup_ref, cw_ref = refs[:6]
    pos = 6
    if not mixer:
        cb_ref = refs[pos]
        pos += 1
    wd_ref = refs[pos]
    pos += 1
    if final_norm:
        fg_ref = refs[pos]
        pos += 1
    o_ref, h_ref = refs[pos:pos + 2]
    u_refs = refs[pos + 2:pos + 4]
    z_refs = refs[pos + 4:pos + 6]

    rows = x_ref.shape[0]
    cols = wd_ref.shape[0] // n_col

    def tile_cols(part, j):
        c0 = (part * n_col + j) * cols
        return slice(c0, c0 + cols)

    n_chunks = rows // CONV_CHUNK
    g = g_ref[...]
    r = pl.program_id(0) % blocks_per_seq
    h_prev = jnp.where(r == 0, 0.0, _rms(xprev_ref[...], g))
    h_next = jnp.where(r == blocks_per_seq - 1, 0.0, _rms(xnext_ref[...], g))
    h_ref[0:HALO, :] = h_prev.astype(BF16)
    h_ref[HALO + rows:, :] = h_next.astype(BF16)
    for c in range(rows // NORM_ROWS):
        x = x_ref[c * NORM_ROWS:(c + 1) * NORM_ROWS, :]
        h_ref[HALO + c * NORM_ROWS:HALO + (c + 1) * NORM_ROWS, :] = _rms(x, g).astype(BF16)
        o_ref[c * NORM_ROWS:(c + 1) * NORM_ROWS, :] = x

    def h_span(c):
        lo = 0 if c == 0 else HALO + c * CONV_CHUNK
        hi = rows + 2 * HALO if c == n_chunks - 1 else HALO + (c + 1) * CONV_CHUNK
        return lo, hi

    def up(j, by_chunk=False):
        spans = [h_span(c) for c in range(n_chunks)] if by_chunk else [(0, rows + 2 * HALO)]
        for lo, hi in spans:
            h = h_ref[lo:hi, :]
            for n in range(n_up):
                u_refs[j % 2][lo:hi, n * cols:(n + 1) * cols] = jnp.dot(
                    h, wup_ref[:, tile_cols(n, j)], preferred_element_type=F32)

    def act(j):
        u = u_refs[j % 2]

        def window(part, r0):
            return u[r0 - SUBLANES:r0 + ACT_ROWS + SUBLANES, part * cols:(part + 1) * cols]

        def conv(win, part):
            n = win.shape[0]
            cw = cw_ref[:, tile_cols(part, j)]
            inner = slice(SUBLANES, SUBLANES + ACT_ROWS)
            return (cw[0:1] * pltpu.roll(win, 1, 0)[inner] + cw[1:2] * win[inner]
                    + cw[2:3] * pltpu.roll(win, n - 1, 0)[inner])

        for c in range(rows // ACT_ROWS):
            r0 = HALO + c * ACT_ROWS
            if mixer:
                gate = u[r0:r0 + ACT_ROWS, 0:cols]
                z = gate * conv(window(1, r0) * window(2, r0), 0)
            else:
                yg = conv(window(0, r0), 0) + cb_ref[:, tile_cols(0, j)]
                yv = conv(window(1, r0), 1) + cb_ref[:, tile_cols(1, j)]
                z = yg * jax.nn.sigmoid(yg) * yv
            group, slot = divmod(j, down_group)
            z_refs[group % 2][c * ACT_ROWS:(c + 1) * ACT_ROWS,
                              slot * cols:(slot + 1) * cols] = z.astype(BF16)

    def down(tiles, row_chunk=None, last=False):
        group = tiles[0] // down_group
        k = len(tiles) * cols
        wd = wd_ref[tiles[0] * cols:tiles[0] * cols + k, :]
        step = rows if row_chunk is None else row_chunk
        for r0 in range(0, rows, step):
            rs = slice(r0, r0 + step)
            o_ref[rs, :] += jnp.dot(z_refs[group % 2][rs, 0:k], wd, preferred_element_type=F32)
            if last and final_norm:
                for p in range(r0, r0 + step, NORM_ROWS):
                    o_ref[p:p + NORM_ROWS, :] = _rms(o_ref[p:p + NORM_ROWS, :], fg_ref[...])

    up(0, by_chunk=True)
    done = 0
    for s in range(n_col):
        if s + 1 < n_col:
            up(s + 1)
        act(s)
        if s >= down_group and s % down_group == 0:
            down(list(range(s - down_group, s)))
            done = s
    down(list(range(done, n_col)), row_chunk=CONV_CHUNK, last=True)


def _gated_block(x2, norm_g, w_up, conv_w, conv_b, w_down, final_g, *, seq, mixer):
    t, d = x2.shape
    hidden = w_down.shape[0]
    n_up = w_up.shape[1] // hidden
    n_conv = conv_w.shape[1] // hidden
    rows, cols = BLOCK_ROWS, BLOCK_COLS
    n_col = hidden // cols
    blocks_per_seq = seq // rows
    halo_per_block = rows // HALO
    last_halo = t // HALO - 1
    final_norm = final_g is not None

    row_spec = pl.BlockSpec((rows, d), lambda i: (i, 0))
    prev_spec = pl.BlockSpec((HALO, d), lambda i: (jnp.maximum(i * halo_per_block - 1, 0), 0))
    next_spec = pl.BlockSpec((HALO, d), lambda i: (jnp.minimum((i + 1) * halo_per_block, last_halo), 0))

    def resident(a):
        zeros = (0,) * a.ndim
        return pl.BlockSpec(a.shape, lambda i: zeros, pipeline_mode=pl.Buffered(1))

    weights = [norm_g.reshape(1, d), w_up.astype(BF16), conv_w]
    if not mixer:
        weights.append(conv_b.reshape(1, n_conv * hidden))
    weights.append(w_down.astype(BF16))
    if final_norm:
        weights.append(final_g.reshape(1, d))

    down_group = 1 if mixer else 5
    scratch = [((rows + 2 * HALO, d), BF16),
               ((rows + 2 * HALO, n_up * cols), F32), ((rows + 2 * HALO, n_up * cols), F32),
               ((rows, down_group * cols), BF16), ((rows, down_group * cols), BF16)]
    vmem = _vmem_limit(
        pipelined=[2 * _nbytes((rows, d), F32), 2 * _nbytes((HALO, d), F32)],
        single=[_nbytes(w.shape, w.dtype) for w in weights] + [_nbytes(*s) for s in scratch]
        + [_nbytes((rows, d), F32)])
    return pl.pallas_call(
        functools.partial(_gated_block_kernel, n_up=n_up, n_col=n_col, mixer=mixer,
                          final_norm=final_norm, blocks_per_seq=blocks_per_seq,
                          down_group=down_group),
        grid=(t // rows,),
        in_specs=[row_spec, prev_spec, next_spec] + [resident(w) for w in weights],
        out_specs=row_spec,
        out_shape=jax.ShapeDtypeStruct((t, d), F32),
        scratch_shapes=[pltpu.VMEM(*s) for s in scratch],
        compiler_params=pltpu.CompilerParams(
            dimension_semantics=("arbitrary",), vmem_limit_bytes=vmem),
        name="mixer_block" if mixer else "conv_ffn",
    )(x2, x2, x2, *weights)


def kernel(x, positions, attn_norm_g, attn_w_qkv, attn_lambda_q1, attn_lambda_k1, attn_lambda_q2, attn_lambda_k2, attn_subln_g, attn_w_o, conv_norm_g, conv_w_in, conv_w, conv_w_out, ffn_norm_g, ffn_w_up, ffn_conv_w, ffn_conv_b, ffn_w_down, final_norm_g):
    batch, seq, d = x.shape
    depth = ffn_norm_g.shape[0]
    assert d == N_HEADS * V_DIM
    x2 = x.reshape(batch * seq, d)
    for i in range(depth):
        if i % N_MIXERS == 0:
            a = i // N_MIXERS
            lambda_init = 0.8 - 0.6 * math.exp(-0.3 * i)
            q, k, v = _qkv_proj(x2, positions, attn_norm_g[a], attn_w_qkv[a])
            x2 = _attention(x2, q, k, v, attn_w_o[a], attn_subln_g[a],
                            attn_lambda_q1[a], attn_lambda_k1[a],
                            attn_lambda_q2[a], attn_lambda_k2[a],
                            seq=seq, lambda_init=lambda_init)
        else:
            c = i // N_MIXERS
            x2 = _gated_block(x2, conv_norm_g[c], conv_w_in[c], conv_w[c], None,
                              conv_w_out[c], None, seq=seq, mixer=True)
        x2 = _gated_block(x2, ffn_norm_g[i], ffn_w_up[i], ffn_conv_w[i], ffn_conv_b[i],
                          ffn_w_down[i], final_norm_g if i == depth - 1 else None,
                          seq=seq, mixer=False)
    return x2.reshape(batch, seq, d)
```
